```python
import math
import jax, jax.numpy as jnp
from jax import lax
import numpy as np

D_MODEL = 4096
BATCH = 32
SEQ = 256
DEPTH = 2
DEC_BATCH = 4
DEC_SEQ = 4096
PAST_LEN = 512

GRID_W = 64
N_EVEN = (DEPTH + 1) // 2
N_ODD = DEPTH // 2
H_A = 16
DH_A = 64
DV_A = 2 * DH_A
H_B = 16
DK_B = 128
DV_B = 128
W_QA = H_A * 2 * DH_A
W_VA = H_A * DV_A
W_QB = H_B * DK_B
W_VB = H_B * DV_B
D_IN_AB = 2 * W_QA + W_VA + 2 * W_QB + 2 * W_VB
SPLITS_AB = (W_QA, 2 * W_QA, 2 * W_QA + W_VA, 2 * W_QA + W_VA + W_QB,
             2 * W_QA + W_VA + 2 * W_QB, 2 * W_QA + W_VA + 2 * W_QB + W_VB)
D_MIX = W_VA + W_VB
CONV_W = 31
FFN_CONV_W = 3
D_FF = 11008
ROPE_BASE = 10000.0
Q_BLOCK = 128
RET_CHUNK = 128
RET_EXP0 = 5.0
EPS = 1e-6
F32 = jnp.float32

kernel_name = "hybrid_diffattn_retention_conformer_prefix_dit_step"


def rms_norm(x, g=None):
    xf = x.astype(F32)
    y = xf * lax.rsqrt(jnp.mean(xf * xf, axis=-1, keepdims=True) + EPS)
    if g is not None:
        y = y * g.astype(F32)
    return y.astype(x.dtype)


def layer_norm(x, g, b):
    xf = x.astype(F32)
    mu = jnp.mean(xf, axis=-1, keepdims=True)
    var = jnp.mean(jnp.square(xf - mu), axis=-1, keepdims=True)
    y = (xf - mu) * lax.rsqrt(var + EPS) * g.astype(F32) + b.astype(F32)
    return y.astype(x.dtype)


def adaln_mods(cond, w, b):
    m = jax.nn.silu(cond) @ w + b
    return [jnp.expand_dims(t, -2) for t in jnp.split(m, 6, axis=-1)]


def modulate(x, g, shift, scale):
    return rms_norm(x, g) * (1 + scale) + shift


def axial_rope_tables(n_tokens, dim):
    rows = n_tokens // GRID_W
    row = jnp.repeat(jnp.arange(rows, dtype=F32), GRID_W)
    col = jnp.tile(jnp.arange(GRID_W, dtype=F32), rows)
    n_freq = dim // 4
    inv = ROPE_BASE ** (-jnp.arange(n_freq, dtype=F32) / n_freq)
    ar = row[:, None] * inv
    ac = col[:, None] * inv
    ang = jnp.concatenate([ar, ar, ac, ac], axis=-1)
    return jnp.cos(ang), jnp.sin(ang)


def apply_rope(x, cos, sin):
    a1, a2, b1, b2 = jnp.split(x, 4, axis=-1)
    rot = jnp.concatenate([-a2, a1, -b2, b1], axis=-1)
    shape = (x.shape[1],) + (1,) * (x.ndim - 3) + (x.shape[-1],)
    return x * cos.reshape(shape).astype(x.dtype) + rot * sin.reshape(shape).astype(x.dtype)


def depthwise_conv(x, w):
    k = w.shape[0]
    return lax.conv_general_dilated(
        x, w[:, None, :].astype(x.dtype), window_strides=(1,),
        padding=[((k - 1) // 2, (k - 1) // 2)],
        dimension_numbers=('NWC', 'WIO', 'NWC'), feature_group_count=x.shape[-1])


def diff_attention(q, k, v, lam):
    b, lq, h, _, dh = q.shape
    nb = lq // Q_BLOCK
    qb = q.reshape(b, nb, Q_BLOCK, h, 2, dh).transpose(1, 0, 2, 3, 4, 5)
    scale = dh ** -0.5

    def block(qblk):
        s = jnp.einsum('bqhcd,bkhcd->bchqk', qblk, k).astype(F32) * scale
        p = jax.nn.softmax(s, axis=-1)
        a = p[:, 0] - lam * p[:, 1]
        return jnp.einsum('bhqk,bkhe->bqhe', a.astype(v.dtype), v)

    o = lax.map(block, qb)
    return o.transpose(1, 0, 2, 3, 4).reshape(b, lq, h, v.shape[-1])


def retention_scan(q, k, v, log_g, s0):
    b, l, h, _ = q.shape
    c = RET_CHUNK
    n = l // c

    def chunks(x):
        return x.reshape(b, n, c, h, x.shape[-1]).transpose(1, 0, 2, 3, 4)

    pos = jnp.arange(c, dtype=F32)
    diff = pos[:, None] - pos[None, :]
    lower = diff >= 0
    d_in = jnp.where(lower[None], jnp.exp(jnp.where(lower, diff, 0.0)[None] * log_g[:, None, None]), 0.0)
    q_dec = jnp.exp((pos[:, None] + 1.0) * log_g[None, :])[..., None]
    k_dec = jnp.exp((c - 1.0 - pos)[:, None] * log_g[None, :])[..., None]
    c_dec = jnp.exp(c * log_g)[:, None, None]

    def step(s, qkv):
        qc, kc, vc = qkv
        att = jnp.einsum('bihd,bjhd->bhij', qc, kc) * d_in
        o = jnp.einsum('bhij,bjhe->bihe', att, vc) + jnp.einsum('bihd,bhde->bihe', qc * q_dec, s)
        s = s * c_dec + jnp.einsum('bjhd,bjhe->bhde', kc * k_dec, vc)
        return s, o

    s_fin, o = lax.scan(step, s0, (chunks(q), chunks(k), chunks(v)))
    return o.transpose(1, 0, 2, 3, 4).reshape(b, l, h, v.shape[-1]), s_fin


def bi_retention(q, k, v, dexp_f, dexp_b, s0_f, s0_b):
    qf, kf, vf = q.astype(F32), k.astype(F32), v.astype(F32)
    lg_f = jnp.log1p(-jnp.exp2(-dexp_f.astype(F32)))
    lg_b = jnp.log1p(-jnp.exp2(-dexp_b.astype(F32)))
    o_f, s_f = retention_scan(qf, kf, vf, lg_f, s0_f.astype(F32))
    flip = lambda t: jnp.flip(t, axis=1)
    o_b, s_b = retention_scan(flip(qf), flip(kf), flip(vf), lg_b, s0_b.astype(F32))
    return o_f + flip(o_b), s_f, s_b


def ab_mixer(h, w_in, w_out, qn_g, kn_g, lq1, lk1, lq2, lk2, subln_g, dexp_f, dexp_b, layer_idx, ctx):
    b, l, _ = h.shape
    qa, ka, va, qb, kb, vb, gb = jnp.split(h @ w_in, list(SPLITS_AB), axis=-1)
    qa = rms_norm(qa.reshape(b, l, H_A, 2, DH_A), qn_g)
    ka = rms_norm(ka.reshape(b, l, H_A, 2, DH_A), kn_g)
    va = va.reshape(b, l, H_A, DV_A)
    qb = qb.reshape(b, l, H_B, DK_B)
    kb = kb.reshape(b, l, H_B, DK_B) * (DK_B ** -0.5)
    vb = vb.reshape(b, l, H_B, DV_B)
    if ctx is None:
        k_all, v_all = ka, va
        s0_f = jnp.zeros((b, H_B, DK_B, DV_B), F32)
        s0_b = s0_f
    else:
        k_c, v_c, s0_f, s0_b = ctx
        cos_a, sin_a = axial_rope_tables(l, DH_A)
        cos_b, sin_b = axial_rope_tables(l, DK_B)
        qa = apply_rope(qa, cos_a, sin_a)
        ka = apply_rope(ka, cos_a, sin_a)
        qb = apply_rope(qb, cos_b, sin_b)
        kb = apply_rope(kb, cos_b, sin_b)
        k_all = jnp.concatenate([k_c.reshape(b, -1, H_A, 2, DH_A).astype(h.dtype), ka], axis=1)
        v_all = jnp.concatenate([v_c.astype(h.dtype), va], axis=1)
    lam_init = 0.8 - 0.6 * math.exp(-0.3 * layer_idx)
    lam = (jnp.exp(jnp.sum(lq1.astype(F32) * lk1.astype(F32)))
           - jnp.exp(jnp.sum(lq2.astype(F32) * lk2.astype(F32))) + lam_init)
    oa = rms_norm(diff_attention(qa, k_all, v_all, lam), subln_g) * (1 - lam_init)
    ob, s_f, s_b = bi_retention(qb, kb, vb, dexp_f, dexp_b, s0_f, s0_b)
    ob = rms_norm(ob).astype(h.dtype).reshape(b, l, W_VB) * jax.nn.silu(gb)
    out = jnp.concatenate([oa.reshape(b, l, W_VA), ob], axis=-1) @ w_out
    if ctx is None:
        return out, (ka.reshape(b, l, H_A, 2 * DH_A), va, s_f, s_b)
    return out, None


def conformer_conv(h, pw1, dw, ln_g, ln_b, pw2):
    a, g = jnp.split(h @ pw1, 2, axis=-1)
    u = depthwise_conv(a * jax.nn.sigmoid(g), dw)
    u = layer_norm(u, ln_g, ln_b)
    return jax.nn.silu(u) @ pw2


def conv_ffn(h, up, dw, down):
    u = depthwise_conv(h @ up, dw)
    g, v = jnp.split(u, 2, axis=-1)
    return (jax.nn.silu(g) * v) @ down


def setup_inputs(seed: int = 0) -> dict:
    key = jax.random.key(seed)
    ks = jax.random.split(key, 32)
    D = D_MODEL

    def nrm(k, shape, scale):
        return jax.random.normal(k, shape, F32) * scale

    def gain(k, shape):
        return 1.0 + nrm(k, shape, 0.05)

    def dexp(k):
        return RET_EXP0 + jnp.arange(H_B, dtype=F32)[None, :] + nrm(k, (N_EVEN, H_B), 0.1)

    return {
        'x_prompt': nrm(ks[0], (BATCH, SEQ, D), 1.0),
        'x_sample': nrm(ks[1], (DEC_BATCH, DEC_SEQ, D), 1.0),
        'cache_k_a': nrm(ks[2], (DEC_BATCH, N_EVEN, PAST_LEN, H_A, 2 * DH_A), 1.0),
        'cache_v_a': nrm(ks[3], (DEC_BATCH, N_EVEN, PAST_LEN, H_A, DV_A), 1.0),
        'state_ret_fwd': nrm(ks[4], (DEC_BATCH, N_EVEN, H_B, DK_B, DV_B), 1.0),
        'state_ret_bwd': nrm(ks[5], (DEC_BATCH, N_EVEN, H_B, DK_B, DV_B), 1.0),
        'c': nrm(ks[6], (DEC_BATCH, D), 1.0),
        'c_ctx': nrm(ks[7], (D,), 1.0),
        'adaln_w': nrm(ks[8], (DEPTH, D, 6 * D), 0.5 * D ** -0.5),
        'adaln_b': nrm(ks[9], (DEPTH, 6 * D), 0.02),
        'norm1_g': gain(ks[10], (DEPTH, D)),
        'norm2_g': gain(ks[11], (DEPTH, D)),
        'w_in_ab': nrm(ks[12], (N_EVEN, D, D_IN_AB), D ** -0.5),
        'w_out_ab': nrm(ks[13], (N_EVEN, D_MIX, D), D_MIX ** -0.5),
        'q_norm_g': gain(ks[14], (N_EVEN, DH_A)),
        'k_norm_g': gain(ks[15], (N_EVEN, DH_A)),
        'lambda_q1': nrm(ks[16], (N_EVEN, DH_A), 0.1),
        'lambda_k1': nrm(ks[17], (N_EVEN, DH_A), 0.1),
        'lambda_q2': nrm(ks[18], (N_EVEN, DH_A), 0.1),
        'lambda_k2': nrm(ks[19], (N_EVEN, DH_A), 0.1),
        'subln_g': gain(ks[20], (N_EVEN, DV_A)),
        'ret_decay_exp_fwd': dexp(ks[21]),
        'ret_decay_exp_bwd': dexp(ks[22]),
        'conv_pw1': nrm(ks[23], (N_ODD, D, 2 * D), D ** -0.5),
        'conv_dw': nrm(ks[24], (N_ODD, CONV_W, D), CONV_W ** -0.5),
        'conv_ln_g': gain(ks[25], (N_ODD, D)),
        'conv_ln_b': nrm(ks[26], (N_ODD, D), 0.02),
        'conv_pw2': nrm(ks[27], (N_ODD, D, D), D ** -0.5),
        'ffn_up': nrm(ks[28], (DEPTH, D, 2 * D_FF), D ** -0.5),
        'ffn_dw': nrm(ks[29], (DEPTH, FFN_CONV_W, 2 * D_FF), FFN_CONV_W ** -0.5),
        'ffn_down': nrm(ks[30], (DEPTH, D_FF, D), D_FF ** -0.5),
    }


def reference(x_prompt, x_sample, cache_k_a, cache_v_a, state_ret_fwd, state_ret_bwd, c, c_ctx,
              adaln_w, adaln_b, norm1_g, norm2_g, w_in_ab, w_out_ab, q_norm_g, k_norm_g,
              lambda_q1, lambda_k1, lambda_q2, lambda_k2, subln_g, ret_decay_exp_fwd, ret_decay_exp_bwd,
              conv_pw1, conv_dw, conv_ln_g, conv_ln_b, conv_pw2, ffn_up, ffn_dw, ffn_down):

    def layer(x, cond, l, ctx):
        sh1, sc1, g1, sh2, sc2, g2 = adaln_mods(cond, adaln_w[l], adaln_b[l])
        h = modulate(x, norm1_g[l], sh1, sc1)
        i = l // 2
        new_ctx = None
        if l % 2 == 0:
            out, new_ctx = ab_mixer(h, w_in_ab[i], w_out_ab[i], q_norm_g[i], k_norm_g[i],
                                    lambda_q1[i], lambda_k1[i], lambda_q2[i], lambda_k2[i], subln_g[i],
                                    ret_decay_exp_fwd[i], ret_decay_exp_bwd[i], l, ctx)
        else:
            out = conformer_conv(h, conv_pw1[i], conv_dw[i], conv_ln_g[i], conv_ln_b[i], conv_pw2[i])
        x = x + g1 * out
        h = modulate(x, norm2_g[l], sh2, sc2)
        x = x + g2 * conv_ffn(h, ffn_up[l], ffn_dw[l], ffn_down[l])
        return x, new_ctx

    x = x_prompt
    ks, vs, sfs, sbs = [], [], [], []
    for l in range(DEPTH):
        x, nc = layer(x, c_ctx, l, None)
        if nc is not None:
            ks.append(nc[0])
            vs.append(nc[1])
            sfs.append(nc[2])
            sbs.append(nc[3])
    y_prompt = x
    new_cache_k_a = jnp.stack(ks, axis=1)
    new_cache_v_a = jnp.stack(vs, axis=1)
    new_state_ret_fwd = jnp.stack(sfs, axis=1)
    new_state_ret_bwd = jnp.stack(sbs, axis=1)

    x = x_sample
    for l in range(DEPTH):
        ctx = None
        if l % 2 == 0:
            i = l // 2
            ctx = (cache_k_a[:, i], cache_v_a[:, i], state_ret_fwd[:, i], state_ret_bwd[:, i])
        x, _ = layer(x, c, l, ctx)
    y_sample = x

    return (y_prompt, y_sample, new_cache_k_a, new_cache_v_a, new_state_ret_fwd, new_state_ret_bwd)
```

```python
import functools
import math

import jax
import jax.numpy as jnp
from jax import lax
from jax.experimental import pallas as pl
from jax.experimental.pallas import tpu as pltpu

F32 = jnp.float32
BF16 = jnp.bfloat16
EPS = 1e-6
GRID_W = 64
ROPE_BASE = 10000.0
LANES = 128
SUBLANES = 8
CONV_HALO = 16
VMEM_LIMIT_V7X = 56 * 1024 * 1024
RET_CHUNK = 256


def _tile(dim, pref, align):
    if dim <= pref:
        return dim
    t = (pref // align) * align
    while t >= align:
        if dim % t == 0:
            return t
        t -= align
    return dim


def _params(*sem):
    return pltpu.CompilerParams(dimension_semantics=sem, vmem_limit_bytes=VMEM_LIMIT_V7X)


def _adaln_kernel(c_ref, w_ref, b_ref, o_ref):
    c = c_ref[...]
    s = (c * jax.nn.sigmoid(c)).astype(BF16)
    w = w_ref[0].astype(BF16)
    o_ref[0] = jnp.dot(s, w, preferred_element_type=F32) + b_ref[0]


def adaln(conds, w, b):
    depth, d, n = w.shape
    tn = _tile(n, 512, LANES)
    return pl.pallas_call(
        _adaln_kernel,
        grid=(depth, n // tn),
        in_specs=[pl.BlockSpec((SUBLANES, d), lambda l, j: (0, 0)),
                  pl.BlockSpec((1, d, tn), lambda l, j: (l, 0, j)),
                  pl.BlockSpec((1, 1, tn), lambda l, j: (l, 0, j))],
        out_specs=pl.BlockSpec((1, SUBLANES, tn), lambda l, j: (l, 0, j)),
        out_shape=jax.ShapeDtypeStruct((depth, SUBLANES, n), F32),
        compiler_params=_params("parallel", "parallel"),
        name="adaln",
    )(conds, w, b.reshape(depth, 1, n))


def _modulate_kernel(x_ref, g_ref, sc_ref, sh_ref, o_ref):
    x = x_ref[0]
    ms = jnp.mean(x * x, axis=-1, keepdims=True)
    y = x * lax.rsqrt(ms + EPS) * g_ref[...]
    o_ref[0] = (y * (1.0 + sc_ref[0]) + sh_ref[0]).astype(o_ref.dtype)


def modulate(x, gain, scale, shift):
    g, t, d = x.shape
    tm = _tile(t, 256, SUBLANES)
    return pl.pallas_call(
        _modulate_kernel,
        grid=(g, t // tm),
        in_specs=[pl.BlockSpec((1, tm, d), lambda b, i: (b, i, 0)),
                  pl.BlockSpec((1, d), lambda b, i: (0, 0)),
                  pl.BlockSpec((1, 1, d), lambda b, i: (b, 0, 0)),
                  pl.BlockSpec((1, 1, d), lambda b, i: (b, 0, 0))],
        out_specs=pl.BlockSpec((1, tm, d), lambda b, i: (b, i, 0)),
        out_shape=jax.ShapeDtypeStruct((g, t, d), BF16),
        compiler_params=_params("parallel", "parallel"),
        name="modulate",
    )(x, gain.reshape(1, d), scale, shift)


def _mm_kernel(a_ref, w_ref, o_ref):
    o_ref[0] = jnp.dot(a_ref[0], w_ref[...], preferred_element_type=F32).astype(o_ref.dtype)


def _mm_res_kernel(a_ref, w_ref, r_ref, g_ref, o_ref):
    acc = jnp.dot(a_ref[0], w_ref[...], preferred_element_type=F32)
    o_ref[0] = r_ref[0] + g_ref[0] * acc


def _mm_glu_kernel(a_ref, wa_ref, wg_ref, o_ref):
    a = jnp.dot(a_ref[0], wa_ref[...], preferred_element_type=F32)
    g = jnp.dot(a_ref[0], wg_ref[...], preferred_element_type=F32)
    o_ref[0] = (a * jax.nn.sigmoid(g)).astype(o_ref.dtype)


def _mm_tiles(t, k, n):
    tm = _tile(t, 1024 if k <= 8192 else 512, SUBLANES)
    tn = _tile(n, 512 if k <= 8192 else 256, LANES)
    return tm, tn


def matmul(a, w, out_dtype):
    g, t, k = a.shape
    n = w.shape[1]
    tm, tn = _mm_tiles(t, k, n)
    return pl.pallas_call(
        _mm_kernel,
        grid=(g, t // tm, n // tn),
        in_specs=[pl.BlockSpec((1, tm, k), lambda b, i, j: (b, i, 0)),
                  pl.BlockSpec((k, tn), lambda b, i, j: (0, j))],
        out_specs=pl.BlockSpec((1, tm, tn), lambda b, i, j: (b, i, j)),
        out_shape=jax.ShapeDtypeStruct((g, t, n), out_dtype),
        compiler_params=_params("parallel", "parallel", "parallel"),
        name="matmul",
    )(a, w)


def matmul_residual(a, w, res, gate):
    g, t, k = a.shape
    n = w.shape[1]
    tm, tn = _mm_tiles(t, k, n)
    return pl.pallas_call(
        _mm_res_kernel,
        grid=(g, t // tm, n // tn),
        in_specs=[pl.BlockSpec((1, tm, k), lambda b, i, j: (b, i, 0)),
                  pl.BlockSpec((k, tn), lambda b, i, j: (0, j)),
                  pl.BlockSpec((1, tm, tn), lambda b, i, j: (b, i, j)),
                  pl.BlockSpec((1, 1, tn), lambda b, i, j: (b, 0, j))],
        out_specs=pl.BlockSpec((1, tm, tn), lambda b, i, j: (b, i, j)),
        out_shape=jax.ShapeDtypeStruct((g, t, n), F32),
        compiler_params=_params("parallel", "parallel", "parallel"),
        name="matmul_residual",
    )(a, w, res, gate)


def matmul_glu(a, w, out_dtype):
    g, t, k = a.shape
    n = w.shape[1] // 2
    tm, tn = _mm_tiles(t, k, n)
    nj = n // tn
    return pl.pallas_call(
        _mm_glu_kernel,
        grid=(g, t // tm, nj),
        in_specs=[pl.BlockSpec((1, tm, k), lambda b, i, j: (b, i, 0)),
                  pl.BlockSpec((k, tn), lambda b, i, j: (0, j)),
                  pl.BlockSpec((k, tn), lambda b, i, j: (0, j + nj))],
        out_specs=pl.BlockSpec((1, tm, tn), lambda b, i, j: (b, i, j)),
        out_shape=jax.ShapeDtypeStruct((g, t, n), out_dtype),
        compiler_params=_params("parallel", "parallel", "parallel"),
        name="matmul_glu",
    )(a, w, w)


def _prep_kernel(*refs, norm, rope_q, prescale, postscale, want_f32, group):
    refs = list(refs)
    x_ref = refs.pop(0)
    if norm:
        g_ref = refs.pop(0)
        bd_ref = refs.pop(0)
    if rope_q:
        cos_ref = refs.pop(0)
        sa_ref = refs.pop(0)
        sb_ref = refs.pop(0)
    o_ref = refs.pop(0)
    f_ref = refs.pop(0) if want_f32 else None
    for hh in range(x_ref.shape[2] // LANES):
        cols = slice(hh * LANES, (hh + 1) * LANES)
        y = x_ref[0, :, cols].astype(F32)
        if norm:
            sq = y * y
            hi = sq.astype(BF16)
            lo = (sq - hi.astype(F32)).astype(BF16)
            ss = (jnp.dot(hi, bd_ref[...], preferred_element_type=F32)
                  + jnp.dot(lo, bd_ref[...], preferred_element_type=F32))
            y = y * lax.rsqrt(ss * (1.0 / group) + EPS) * g_ref[...]
        if prescale != 1.0:
            y = y * prescale
        if rope_q:
            y = (y * cos_ref[...]
                 + pltpu.roll(y, LANES - rope_q, axis=1) * sa_ref[...]
                 + pltpu.roll(y, rope_q, axis=1) * sb_ref[...])
        if want_f32:
            f_ref[0, :, cols] = y
        if postscale != 1.0:
            y = y * postscale
        o_ref[0, :, cols] = y.astype(o_ref.dtype)


def head_prep(proj, col_start, width, *, gain=None, group=None, rope=None, prescale=1.0,
              postscale=1.0, want_f32=False):
    g, t, _ = proj.shape
    cw = _tile(width, 1024, LANES)
    tm = _tile(t, 512, SUBLANES)
    c0 = col_start // cw
    assert col_start % cw == 0
    in_specs = [pl.BlockSpec((1, tm, cw), lambda b, i, j: (b, i, c0 + j))]
    args = [proj]
    if gain is not None:
        lane = jnp.arange(LANES)
        bd = (lane[:, None] // group == lane[None, :] // group).astype(BF16)
        in_specs += [pl.BlockSpec((1, LANES), lambda b, i, j: (0, 0)),
                     pl.BlockSpec((LANES, LANES), lambda b, i, j: (0, 0))]
        args += [jnp.tile(gain.astype(F32), LANES // group).reshape(1, LANES), bd]
    rope_q = 0
    if rope is not None:
        rope_q, cos, sin_a, sin_b = rope
        in_specs += [pl.BlockSpec((tm, LANES), lambda b, i, j: (i, 0))] * 3
        args += [cos, sin_a, sin_b]
    out_specs = [pl.BlockSpec((1, tm, cw), lambda b, i, j: (b, i, j))]
    out_shape = [jax.ShapeDtypeStruct((g, t, width), BF16)]
    if want_f32:
        out_specs.append(pl.BlockSpec((1, tm, cw), lambda b, i, j: (b, i, j)))
        out_shape.append(jax.ShapeDtypeStruct((g, t, width), F32))
    outs = pl.pallas_call(
        functools.partial(_prep_kernel, norm=gain is not None, rope_q=rope_q, prescale=prescale,
                          postscale=postscale, want_f32=want_f32, group=group),
        grid=(g, t // tm, width // cw),
        in_specs=in_specs,
        out_specs=out_specs,
        out_shape=out_shape,
        compiler_params=_params("parallel", "parallel", "parallel"),
        name="head_prep",
    )(*args)
    return outs if want_f32 else outs[0]


def rope_tables(n_tokens, dim):
    rows = n_tokens // GRID_W
    row = jnp.repeat(jnp.arange(rows, dtype=F32), GRID_W)
    col = jnp.tile(jnp.arange(GRID_W, dtype=F32), rows)
    n_freq = dim // 4
    inv = ROPE_BASE ** (-jnp.arange(n_freq, dtype=F32) / n_freq)
    ar = row[:, None] * inv
    ac = col[:, None] * inv
    ang = jnp.concatenate([ar, ar, ac, ac], axis=-1)
    cos, sin = jnp.cos(ang), jnp.sin(ang)
    even = ((jnp.arange(dim) // n_freq) % 2 == 0)[None, :]
    sin_a = jnp.where(even, -sin, 0.0)
    sin_b = jnp.where(even, 0.0, sin)
    rep = LANES // dim
    return n_freq, jnp.tile(cos, (1, rep)), jnp.tile(sin_a, (1, rep)), jnp.tile(sin_b, (1, rep))


def _attn_kernel(q_ref, k_ref, v_ref, lp_ref, g_ref, o_ref, qs_ref, m_ref, l_ref, acc_ref, *,
                 lam_init, dh):
    kv = pl.program_id(3)
    tq = q_ref.shape[1]

    @pl.when(kv == 0)
    def _():
        q = q_ref[0]
        lane = lax.broadcasted_iota(jnp.int32, q.shape, 1)
        zero = jnp.zeros_like(q)
        qs_ref[0:tq, :] = jnp.where(lane < dh, q, zero)
        qs_ref[tq:2 * tq, :] = jnp.where(lane >= dh, q, zero)
        m_ref[...] = jnp.full(m_ref.shape, -jnp.inf, F32)
        l_ref[...] = jnp.zeros(l_ref.shape, F32)
        acc_ref[...] = jnp.zeros(acc_ref.shape, F32)

    s = lax.dot_general(qs_ref[...], k_ref[0], (((1,), (1,)), ((), ())),
                        preferred_element_type=F32)
    m_prev = m_ref[...]
    m_new = jnp.maximum(m_prev, jnp.max(s, axis=1, keepdims=True))
    alpha = jnp.exp(m_prev - m_new)
    p = jnp.exp(s - m_new)
    l_ref[...] = alpha * l_ref[...] + jnp.sum(p, axis=1, keepdims=True)
    acc_ref[...] = alpha * acc_ref[...] + jnp.dot(p.astype(BF16), v_ref[0],
                                                   preferred_element_type=F32)
    m_ref[...] = m_new

    @pl.when(kv == pl.num_programs(3) - 1)
    def _():
        lp = lp_ref[...]
        lam = (jnp.exp(jnp.sum(lp[0:1] * lp[1:2], axis=1, keepdims=True))
               - jnp.exp(jnp.sum(lp[2:3] * lp[3:4], axis=1, keepdims=True)) + lam_init)
        on = acc_ref[...] / l_ref[...]
        o = on[0:tq] - lam * on[tq:2 * tq]
        ms = jnp.mean(o * o, axis=-1, keepdims=True)
        o = o * lax.rsqrt(ms + EPS) * g_ref[...] * (1.0 - lam_init)
        o_ref[0] = o.astype(o_ref.dtype)


def diff_attention(q, k, v, lam_params, subln_g, lam_init, dh):
    b, lq, w = q.shape
    lk = k.shape[1]
    h = w // LANES
    tq = _tile(lq, 512, SUBLANES)
    tk = _tile(lk, 512, LANES)
    return pl.pallas_call(
        functools.partial(_attn_kernel, lam_init=lam_init, dh=dh),
        grid=(b, h, lq // tq, lk // tk),
        in_specs=[pl.BlockSpec((1, tq, LANES), lambda bi, hi, qi, ki: (bi, qi, hi)),
                  pl.BlockSpec((1, tk, LANES), lambda bi, hi, qi, ki: (bi, ki, hi)),
                  pl.BlockSpec((1, tk, LANES), lambda bi, hi, qi, ki: (bi, ki, hi)),
                  pl.BlockSpec((4, dh), lambda bi, hi, qi, ki: (0, 0)),
                  pl.BlockSpec((1, LANES), lambda bi, hi, qi, ki: (0, 0))],
        out_specs=pl.BlockSpec((1, tq, LANES), lambda bi, hi, qi, ki: (bi, qi, hi)),
        out_shape=jax.ShapeDtypeStruct((b, lq, w), BF16),
        scratch_shapes=[pltpu.VMEM((2 * tq, LANES), BF16),
                        pltpu.VMEM((2 * tq, 1), F32),
                        pltpu.VMEM((2 * tq, 1), F32),
                        pltpu.VMEM((2 * tq, LANES), F32)],
        compiler_params=_params("parallel", "parallel", "parallel", "arbitrary"),
        name="diff_attention",
    )(q, k, v, lam_params, subln_g.astype(F32).reshape(1, LANES))


def _retention_kernel(q_ref, k_ref, v_ref, gate_ref, dx_ref, s0f_ref, s0b_ref,
                      o_ref, sf_ref, sb_ref, acc_ref, *, chunk):
    seq = q_ref.shape[1]
    n_chunks = seq // chunk
    dx = dx_ref[0]
    lg = jnp.log1p(-jnp.exp2(-dx))
    lgf, lgb = lg[0:1], lg[1:2]
    ii = lax.broadcasted_iota(jnp.int32, (chunk, chunk), 0)
    jj = lax.broadcasted_iota(jnp.int32, (chunk, chunk), 1)
    d = (ii - jj).astype(F32)
    dmat = (jnp.where(d >= 0, jnp.exp(jnp.maximum(d, 0.0) * lgf[:, 0:1]), 0.0)
            + jnp.where(d <= 0, jnp.exp(jnp.maximum(-d, 0.0) * lgb[:, 0:1]), 0.0))
    pos = lax.broadcasted_iota(jnp.int32, (chunk, LANES), 0).astype(F32)
    qdec_f = jnp.exp((pos + 1.0) * lgf)
    kdec_f = jnp.exp((chunk - 1.0 - pos) * lgf)
    cdec_f = jnp.exp(chunk * lgf)
    qdec_b = jnp.exp((chunk - pos) * lgb)
    kdec_b = jnp.exp(pos * lgb)
    cdec_b = jnp.exp(chunk * lgb)
    dim0 = (((0,), (0,)), ((), ()))
    dim1 = (((1,), (1,)), ((), ()))

    def fwd(n, s):
        rows = pl.ds(pl.multiple_of(n * chunk, chunk), chunk)
        q, k, v = q_ref[0, rows, :], k_ref[0, rows, :], v_ref[0, rows, :]
        att = lax.dot_general(q, k, dim1, preferred_element_type=F32) * dmat
        acc_ref[rows, :] = (jnp.dot(att.astype(BF16), v, preferred_element_type=F32)
                            + jnp.dot((q * qdec_f).astype(BF16), s.astype(BF16),
                                      preferred_element_type=F32))
        return s * cdec_f + lax.dot_general((k * kdec_f).astype(BF16), v, dim0,
                                            preferred_element_type=F32)

    sf_ref[0, 0] = lax.fori_loop(0, n_chunks, fwd, s0f_ref[0, 0])

    def bwd(m, s):
        n = n_chunks - 1 - m
        rows = pl.ds(pl.multiple_of(n * chunk, chunk), chunk)
        q, k, v = q_ref[0, rows, :], k_ref[0, rows, :], v_ref[0, rows, :]
        acc_ref[rows, :] += jnp.dot((q * qdec_b).astype(BF16), s.astype(BF16),
                                    preferred_element_type=F32)
        return s * cdec_b + lax.dot_general((k * kdec_b).astype(BF16), v, dim0,
                                            preferred_element_type=F32)

    sb_ref[0, 0] = lax.fori_loop(0, n_chunks, bwd, s0b_ref[0, 0])

    o = acc_ref[...]
    ms = jnp.mean(o * o, axis=-1, keepdims=True)
    gate = gate_ref[0].astype(F32)
    o_ref[0] = (o * lax.rsqrt(ms + EPS) * (gate * jax.nn.sigmoid(gate))).astype(o_ref.dtype)


def bi_retention(q, k, v, proj, gate_col, dexp, s0f, s0b):
    b, seq, w = q.shape
    h = w // LANES
    chunk = _tile(seq, RET_CHUNK, LANES)
    gc = gate_col // LANES
    head = lambda bi, hi: (bi, 0, hi)
    state = lambda bi, hi: (bi, hi, 0, 0)
    return pl.pallas_call(
        functools.partial(_retention_kernel, chunk=chunk),
        grid=(b, h),
        in_specs=[pl.BlockSpec((1, seq, LANES), head),
                  pl.BlockSpec((1, seq, LANES), head),
                  pl.BlockSpec((1, seq, LANES), head),
                  pl.BlockSpec((1, seq, LANES), lambda bi, hi: (bi, 0, gc + hi)),
                  pl.BlockSpec((1, 2, LANES), lambda bi, hi: (hi, 0, 0)),
                  pl.BlockSpec((1, 1, LANES, LANES), state),
                  pl.BlockSpec((1, 1, LANES, LANES), state)],
        out_specs=[pl.BlockSpec((1, seq, LANES), head),
                   pl.BlockSpec((1, 1, LANES, LANES), state),
                   pl.BlockSpec((1, 1, LANES, LANES), state)],
        out_shape=[jax.ShapeDtypeStruct((b, seq, w), BF16),
                   jax.ShapeDtypeStruct((b, h, LANES, LANES), F32),
                   jax.ShapeDtypeStruct((b, h, LANES, LANES), F32)],
        scratch_shapes=[pltpu.VMEM((seq, LANES), F32)],
        compiler_params=_params("parallel", "parallel"),
        name="bi_retention",
    )(q, k, v, proj, dexp, s0f, s0b)


def _conv_ln_kernel(u_ref, prev_ref, next_ref, w_ref, g_ref, b_ref, o_ref, ext_ref, y_ref, *,
                    seq_len, taps, row_block):
    i = pl.program_id(1)
    c = pl.program_id(2)
    tm, cw = u_ref.shape[1], u_ref.shape[2]
    first = (i * tm) % seq_len == 0
    last = ((i + 1) * tm) % seq_len == 0
    zero = jnp.zeros((CONV_HALO, cw), F32)
    ext_ref[0:CONV_HALO, :] = jnp.where(first, zero, prev_ref[0])
    ext_ref[CONV_HALO:CONV_HALO + tm, :] = u_ref[0]
    ext_ref[CONV_HALO + tm:, :] = jnp.where(last, zero, next_ref[0])
    half = (taps - 1) // 2
    for r0 in range(0, tm, row_block):
        acc = jnp.zeros((row_block, cw), F32)
        for k in range(taps):
            acc = acc + ext_ref[pl.ds(CONV_HALO + r0 - half + k, row_block), :] * w_ref[k:k + 1, :]
        y_ref[c, pl.ds(r0, row_block), :] = acc

    @pl.when(c == pl.num_programs(2) - 1)
    def _():
        nc = y_ref.shape[0]
        d = nc * cw
        mu = sum(jnp.sum(y_ref[cc], axis=-1, keepdims=True) for cc in range(nc)) * (1.0 / d)
        var = sum(jnp.sum(jnp.square(y_ref[cc] - mu), axis=-1, keepdims=True)
                  for cc in range(nc)) * (1.0 / d)
        inv = lax.rsqrt(var + EPS)
        for cc in range(nc):
            cols = slice(cc * cw, (cc + 1) * cw)
            z = (y_ref[cc] - mu) * inv * g_ref[:, cols] + b_ref[:, cols]
            o_ref[0, :, cols] = (z * jax.nn.sigmoid(z)).astype(o_ref.dtype)


def conv_ln_silu(u, dw, ln_g, ln_b, seq_len):
    g, t, d = u.shape
    taps = dw.shape[0]
    assert (taps - 1) // 2 < CONV_HALO
    tm = _tile(min(t, seq_len), 256, CONV_HALO)
    cw = _tile(d, 512, LANES)
    nh = tm // CONV_HALO
    n_halo_blocks = t // CONV_HALO
    w = jnp.zeros((2 * CONV_HALO, d), F32).at[:taps].set(dw.astype(F32))
    return pl.pallas_call(
        functools.partial(_conv_ln_kernel, seq_len=seq_len, taps=taps, row_block=min(tm, 32)),
        grid=(g, t // tm, d // cw),
        in_specs=[pl.BlockSpec((1, tm, cw), lambda b, i, c: (b, i, c)),
                  pl.BlockSpec((1, CONV_HALO, cw), lambda b, i, c: (b, jnp.maximum(i * nh - 1, 0), c)),
                  pl.BlockSpec((1, CONV_HALO, cw),
                               lambda b, i, c: (b, jnp.minimum((i + 1) * nh, n_halo_blocks - 1), c)),
                  pl.BlockSpec((2 * CONV_HALO, cw), lambda b, i, c: (0, c)),
                  pl.BlockSpec((1, d), lambda b, i, c: (0, 0)),
                  pl.BlockSpec((1, d), lambda b, i, c: (0, 0))],
        out_specs=pl.BlockSpec((1, tm, d), lambda b, i, c: (b, i, 0)),
        out_shape=jax.ShapeDtypeStruct((g, t, d), BF16),
        scratch_shapes=[pltpu.VMEM((tm + 2 * CONV_HALO, cw), F32),
                        pltpu.VMEM((d // cw, tm, cw), F32)],
        compiler_params=_params("parallel", "parallel", "arbitrary"),
        name="conv_ln_silu",
    )(u, u, u, w, ln_g.astype(F32).reshape(1, d), ln_b.astype(F32).reshape(1, d))


def _conv3(x, prev_row, next_row, w, row):
    tm = x.shape[0]
    xm = jnp.where(row == 0, prev_row, pltpu.roll(x, 1, axis=0))
    xp = jnp.where(row == tm - 1, next_row, pltpu.roll(x, tm - 1, axis=0))
    return xm * w[0:1] + x * w[1:2] + xp * w[2:3]


FFN_HALO = 16


def _ffn_glu_kernel(pg_ref, pgp_ref, pgn_ref, pv_ref, pvp_ref, pvn_ref, wg_ref, wv_ref, o_ref, *,
                    seq_len):
    i = pl.program_id(1)
    tm, tf = pg_ref.shape[1], pg_ref.shape[2]
    first = (i * tm) % seq_len == 0
    last = ((i + 1) * tm) % seq_len == 0
    row = lax.broadcasted_iota(jnp.int32, (tm, tf), 0)
    zero = jnp.zeros((1, tf), F32)

    def conv(main_ref, prev_ref, next_ref, w_ref):
        prev_row = jnp.where(first, zero, prev_ref[0].astype(F32)[FFN_HALO - 1:FFN_HALO, :])
        next_row = jnp.where(last, zero, next_ref[0].astype(F32)[0:1, :])
        return _conv3(main_ref[0].astype(F32), prev_row, next_row, w_ref[...], row)

    gate = conv(pg_ref, pgp_ref, pgn_ref, wg_ref)
    val = conv(pv_ref, pvp_ref, pvn_ref, wv_ref)
    o_ref[0] = (gate * jax.nn.sigmoid(gate) * val).astype(o_ref.dtype)


def ffn_conv_glu(p, dw, seq_len):
    g, t, f2 = p.shape
    f = f2 // 2
    assert dw.shape[0] == 3
    tm = _tile(min(t, seq_len), 128, FFN_HALO)
    tf = _tile(f, 6144, LANES)
    nj = f // tf
    nh = tm // FFN_HALO
    n_halo_blocks = t // FFN_HALO
    w = jnp.zeros((SUBLANES, f2), F32).at[:3].set(dw.astype(F32))
    main = lambda off: pl.BlockSpec((1, tm, tf), lambda b, i, j: (b, i, j + off))
    prev = lambda off: pl.BlockSpec(
        (1, FFN_HALO, tf), lambda b, i, j: (b, jnp.maximum(i * nh - 1, 0), j + off))
    nxt = lambda off: pl.BlockSpec(
        (1, FFN_HALO, tf), lambda b, i, j: (b, jnp.minimum((i + 1) * nh, n_halo_blocks - 1), j + off))
    return pl.pallas_call(
        functools.partial(_ffn_glu_kernel, seq_len=seq_len),
        grid=(g, t // tm, nj),
        in_specs=[main(0), prev(0), nxt(0), main(nj), prev(nj), nxt(nj),
                  pl.BlockSpec((SUBLANES, tf), lambda b, i, j: (0, j)),
                  pl.BlockSpec((SUBLANES, tf), lambda b, i, j: (0, j + nj))],
        out_specs=pl.BlockSpec((1, tm, tf), lambda b, i, j: (b, i, j)),
        out_shape=jax.ShapeDtypeStruct((g, t, f), BF16),
        compiler_params=_params("parallel", "parallel", "parallel"),
        name="ffn_conv_glu",
    )(p, p, p, p, p, p, w, w)


def _mods(mods_l, rows, d):
    m = mods_l[rows[0]:rows[1]]
    return [m[:, k * d:(k + 1) * d].reshape(-1, 1, d) for k in range(6)]


def _ffn_block(x, mods, norm_g, up, dw, down, seq_len):
    sh2, sc2, g2 = mods
    h = modulate(x, norm_g, sc2, sh2)
    p = matmul(h, up, BF16)
    a = ffn_conv_glu(p, dw, seq_len)
    return matmul_residual(a, down, x, g2)


def kernel(x_prompt, x_sample, cache_k_a, cache_v_a, state_ret_fwd, state_ret_bwd, c, c_ctx, adaln_w, adaln_b, norm1_g, norm2_g, w_in_ab, w_out_ab, q_norm_g, k_norm_g, lambda_q1, lambda_k1, lambda_q2, lambda_k2, subln_g, ret_decay_exp_fwd, ret_decay_exp_bwd, conv_pw1, conv_dw, conv_ln_g, conv_ln_b, conv_pw2, ffn_up, ffn_dw, ffn_down):
    batch, seq, d = x_prompt.shape
    dec_batch, dec_seq, _ = x_sample.shape
    depth = adaln_w.shape[0]
    _, _, past_len, h_a, dk_a = cache_k_a.shape
    dh_a = dk_a // 2
    dv_a = cache_v_a.shape[-1]
    _, _, h_b, dk_b, dv_b = state_ret_fwd.shape
    assert dk_a == LANES and dv_a == LANES and dk_b == LANES and dv_b == LANES
    assert dec_batch + 1 <= SUBLANES
    w_qa, w_va, w_qb, w_vb = h_a * dk_a, h_a * dv_a, h_b * dk_b, h_b * dv_b
    col = {}
    off = 0
    for name, width in (("qa", w_qa), ("ka", w_qa), ("va", w_va), ("qb", w_qb), ("kb", w_qb),
                        ("vb", w_vb), ("gb", w_vb)):
        col[name] = off
        off += width

    conds = jnp.zeros((SUBLANES, d), F32).at[0].set(c_ctx).at[1:1 + dec_batch].set(c)
    mods_all = adaln(conds, adaln_w, adaln_b)

    w_in_bf = w_in_ab.astype(BF16)
    w_out_bf = w_out_ab.astype(BF16)
    pw1_bf = conv_pw1.astype(BF16)
    pw2_bf = conv_pw2.astype(BF16)
    up_bf = ffn_up.astype(BF16)
    down_bf = ffn_down.astype(BF16)

    rope_a = rope_tables(dec_seq, dh_a)
    rope_b = rope_tables(dec_seq, dk_b)

    def run(x, rows, seq_len, latent):
        g, t, _ = x.shape
        n_seq = g * t // seq_len
        new_ctx = []
        for l in range(depth):
            sh1, sc1, g1, sh2, sc2, g2 = _mods(mods_all[l], rows, d)
            h = modulate(x, norm1_g[l], sc1, sh1)
            i = l // 2
            if l % 2 == 0:
                proj = matmul(h, w_in_bf[i], F32)
                ra = rope_a if latent else None
                rb = rope_b if latent else None
                qa = head_prep(proj, col["qa"], w_qa, gain=q_norm_g[i], group=dh_a, rope=ra,
                               postscale=dh_a ** -0.5)
                ka = head_prep(proj, col["ka"], w_qa, gain=k_norm_g[i], group=dh_a, rope=ra,
                               want_f32=not latent)
                va = head_prep(proj, col["va"], w_va, want_f32=not latent)
                qb = head_prep(proj, col["qb"], w_qb, rope=rb)
                kb = head_prep(proj, col["kb"], w_qb, rope=rb, prescale=dk_b ** -0.5)
                vb = head_prep(proj, col["vb"], w_vb)
                seqs = lambda z: z.reshape(n_seq, seq_len, z.shape[-1])
                if latent:
                    k_all = jnp.concatenate(
                        [cache_k_a[:, i].reshape(dec_batch, past_len, w_qa).astype(BF16), ka], axis=1)
                    v_all = jnp.concatenate(
                        [cache_v_a[:, i].reshape(dec_batch, past_len, w_va).astype(BF16), va], axis=1)
                    s0f, s0b = state_ret_fwd[:, i], state_ret_bwd[:, i]
                else:
                    ka, ka_f32 = ka
                    va, va_f32 = va
                    k_all, v_all = seqs(ka), seqs(va)
                    s0f = jnp.zeros((n_seq, h_b, dk_b, dv_b), F32)
                    s0b = s0f
                lam_init = 0.8 - 0.6 * math.exp(-0.3 * l)
                lam_params = jnp.stack([lambda_q1[i], lambda_k1[i], lambda_q2[i], lambda_k2[i]]).astype(F32)
                oa = diff_attention(seqs(qa), k_all, v_all, lam_params, subln_g[i], lam_init, dh_a)
                dexp = jnp.stack([ret_decay_exp_fwd[i], ret_decay_exp_bwd[i]], axis=1).astype(F32)
                dexp = jnp.broadcast_to(dexp[:, :, None], (h_b, 2, LANES))
                ob, s_f, s_b = bi_retention(seqs(qb), seqs(kb), seqs(vb), seqs(proj), col["gb"], dexp,
                                            s0f, s0b)
                mix = jnp.concatenate([oa.reshape(g, t, w_va), ob.reshape(g, t, w_vb)], axis=-1)
                x = matmul_residual(mix, w_out_bf[i], x, g1)
                if not latent:
                    new_ctx.append((ka_f32.reshape(n_seq, seq_len, h_a, dk_a),
                                    va_f32.reshape(n_seq, seq_len, h_a, dv_a), s_f, s_b))
            else:
                u = matmul_glu(h, pw1_bf[i], F32)
                z = conv_ln_silu(u, conv_dw[i], conv_ln_g[i], conv_ln_b[i], seq_len)
                x = matmul_residual(z, pw2_bf[i], x, g1)
            x = _ffn_block(x, (sh2, sc2, g2), norm2_g[l], up_bf[l], ffn_dw[l], down_bf[l], seq_len)
        return x, new_ctx

    y_ctx, new_ctx = run(x_prompt.reshape(1, batch * seq, d), (0, 1), seq, False)
    y_lat, _ = run(x_sample, (1, 1 + dec_batch), dec_seq, True)
    stack = lambda k: jnp.stack([n[k] for n in new_ctx], axis=1)
    return (y_ctx.reshape(batch, seq, d), y_lat, stack(0), stack(1), stack(2), stack(3))
```

```python
import functools
import math

import jax
import jax.numpy as jnp
from jax import lax
from jax.experimental import pallas as pl
from jax.experimental.pallas import tpu as pltpu

F32 = jnp.float32
BF16 = jnp.bfloat16
EPS = 1e-6
GRID_W = 64
ROPE_BASE = 10000.0
LANES = 128
SUBLANES = 8
MXU_COLS = 256
CONV_HALO = 16
VMEM_LIMIT_V7X = 56 * 1024 * 1024
RET_CHUNK = 256


def _tile(dim, pref, align):
    if dim <= pref:
        return dim
    t = (pref // align) * align
    while t >= align:
        if dim % t == 0:
            return t
        t -= align
    return dim


def _params(*sem):
    return pltpu.CompilerParams(dimension_semantics=sem, vmem_limit_bytes=VMEM_LIMIT_V7X)


def _adaln_kernel(c_ref, w_ref, b_ref, o_ref):
    c = c_ref[...]
    s = (c * jax.nn.sigmoid(c)).astype(BF16)
    w = w_ref[0].astype(BF16)
    o_ref[0] = jnp.dot(s, w, preferred_element_type=F32) + b_ref[0]


def adaln(conds, w, b):
    depth, d, n = w.shape
    tn = _tile(n, 512, LANES)
    return pl.pallas_call(
        _adaln_kernel,
        grid=(depth, n // tn),
        in_specs=[pl.BlockSpec((SUBLANES, d), lambda l, j: (0, 0)),
                  pl.BlockSpec((1, d, tn), lambda l, j: (l, 0, j)),
                  pl.BlockSpec((1, 1, tn), lambda l, j: (l, 0, j))],
        out_specs=pl.BlockSpec((1, SUBLANES, tn), lambda l, j: (l, 0, j)),
        out_shape=jax.ShapeDtypeStruct((depth, SUBLANES, n), F32),
        compiler_params=_params("parallel", "parallel"),
        name="adaln",
    )(conds, w, b.reshape(depth, 1, n))


def _modulate_kernel(x_ref, g_ref, sc_ref, sh_ref, o_ref):
    x = x_ref[0]
    ms = jnp.mean(x * x, axis=-1, keepdims=True)
    y = x * lax.rsqrt(ms + EPS) * g_ref[...]
    o_ref[0] = (y * (1.0 + sc_ref[0]) + sh_ref[0]).astype(o_ref.dtype)


def modulate(x, gain, scale, shift):
    g, t, d = x.shape
    tm = _tile(t, 256, SUBLANES)
    return pl.pallas_call(
        _modulate_kernel,
        grid=(g, t // tm),
        in_specs=[pl.BlockSpec((1, tm, d), lambda b, i: (b, i, 0)),
                  pl.BlockSpec((1, d), lambda b, i: (0, 0)),
                  pl.BlockSpec((1, 1, d), lambda b, i: (b, 0, 0)),
                  pl.BlockSpec((1, 1, d), lambda b, i: (b, 0, 0))],
        out_specs=pl.BlockSpec((1, tm, d), lambda b, i: (b, i, 0)),
        out_shape=jax.ShapeDtypeStruct((g, t, d), BF16),
        compiler_params=_params("parallel", "parallel"),
        name="modulate",
    )(x, gain.reshape(1, d), scale, shift)


def _mm_kernel(a_ref, w_ref, o_ref):
    o_ref[0] = jnp.dot(a_ref[0], w_ref[...], preferred_element_type=F32).astype(o_ref.dtype)


def _mm_res_kernel(a_ref, w_ref, r_ref, g_ref, o_ref):
    acc = jnp.dot(a_ref[0], w_ref[...], preferred_element_type=F32)
    o_ref[0] = r_ref[0] + g_ref[0] * acc


def _mm_glu_kernel(a_ref, wa_ref, wg_ref, o_ref):
    a = jnp.dot(a_ref[0], wa_ref[...], preferred_element_type=F32)
    g = jnp.dot(a_ref[0], wg_ref[...], preferred_element_type=F32)
    o_ref[0] = (a * jax.nn.sigmoid(g)).astype(o_ref.dtype)


def _mm_tiles(t, k, n):
    tm = _tile(t, 1024 if k <= 8192 else 512, SUBLANES)
    tn = _tile(n, 512 if k <= 8192 else 256, LANES)
    return tm, tn


def matmul(a, w, out_dtype):
    g, t, k = a.shape
    n = w.shape[1]
    tm, tn = _mm_tiles(t, k, n)
    return pl.pallas_call(
        _mm_kernel,
        grid=(g, t // tm, n // tn),
        in_specs=[pl.BlockSpec((1, tm, k), lambda b, i, j: (b, i, 0)),
                  pl.BlockSpec((k, tn), lambda b, i, j: (0, j))],
        out_specs=pl.BlockSpec((1, tm, tn), lambda b, i, j: (b, i, j)),
        out_shape=jax.ShapeDtypeStruct((g, t, n), out_dtype),
        compiler_params=_params("parallel", "parallel", "parallel"),
        name="matmul",
    )(a, w)


def matmul_residual(a, w, res, gate):
    g, t, k = a.shape
    n = w.shape[1]
    tm, tn = _mm_tiles(t, k, n)
    return pl.pallas_call(
        _mm_res_kernel,
        grid=(g, t // tm, n // tn),
        in_specs=[pl.BlockSpec((1, tm, k), lambda b, i, j: (b, i, 0)),
                  pl.BlockSpec((k, tn), lambda b, i, j: (0, j)),
                  pl.BlockSpec((1, tm, tn), lambda b, i, j: (b, i, j)),
                  pl.BlockSpec((1, 1, tn), lambda b, i, j: (b, 0, j))],
        out_specs=pl.BlockSpec((1, tm, tn), lambda b, i, j: (b, i, j)),
        out_shape=jax.ShapeDtypeStruct((g, t, n), F32),
        compiler_params=_params("parallel", "parallel", "parallel"),
        name="matmul_residual",
    )(a, w, res, gate)


def matmul_glu(a, w, out_dtype):
    g, t, k = a.shape
    n = w.shape[1] // 2
    tm, tn = _mm_tiles(t, k, n)
    nj = n // tn
    return pl.pallas_call(
        _mm_glu_kernel,
        grid=(g, t // tm, nj),
        in_specs=[pl.BlockSpec((1, tm, k), lambda b, i, j: (b, i, 0)),
                  pl.BlockSpec((k, tn), lambda b, i, j: (0, j)),
                  pl.BlockSpec((k, tn), lambda b, i, j: (0, j + nj))],
        out_specs=pl.BlockSpec((1, tm, tn), lambda b, i, j: (b, i, j)),
        out_shape=jax.ShapeDtypeStruct((g, t, n), out_dtype),
        compiler_params=_params("parallel", "parallel", "parallel"),
        name="matmul_glu",
    )(a, w, w)


def _prep_kernel(*refs, norm, rope_q, prescale, postscale, want_f32, group):
    refs = list(refs)
    x_ref = refs.pop(0)
    if norm:
        g_ref = refs.pop(0)
        bd_ref = refs.pop(0)
    if rope_q:
        cos_ref = refs.pop(0)
        sa_ref = refs.pop(0)
        sb_ref = refs.pop(0)
    o_ref = refs.pop(0)
    f_ref = refs.pop(0) if want_f32 else None
    for hh in range(x_ref.shape[2] // LANES):
        cols = slice(hh * LANES, (hh + 1) * LANES)
        y = x_ref[0, :, cols].astype(F32)
        if norm:
            sq = y * y
            hi = sq.astype(BF16)
            lo = (sq - hi.astype(F32)).astype(BF16)
            ss = (jnp.dot(hi, bd_ref[...], preferred_element_type=F32)
                  + jnp.dot(lo, bd_ref[...], preferred_element_type=F32))
            y = y * lax.rsqrt(ss * (1.0 / group) + EPS) * g_ref[...]
        if prescale != 1.0:
            y = y * prescale
        if rope_q:
            y = (y * cos_ref[...]
                 + pltpu.roll(y, LANES - rope_q, axis=1) * sa_ref[...]
                 + pltpu.roll(y, rope_q, axis=1) * sb_ref[...])
        if want_f32:
            f_ref[0, :, cols] = y
        if postscale != 1.0:
            y = y * postscale
        o_ref[0, :, cols] = y.astype(o_ref.dtype)


def head_prep(proj, col_start, width, *, gain=None, group=None, rope=None, prescale=1.0,
              postscale=1.0, want_f32=False):
    g, t, _ = proj.shape
    cw = _tile(width, 1024, LANES)
    tm = _tile(t, 512, SUBLANES)
    c0 = col_start // cw
    assert col_start % cw == 0
    in_specs = [pl.BlockSpec((1, tm, cw), lambda b, i, j: (b, i, c0 + j))]
    args = [proj]
    if gain is not None:
        lane = jnp.arange(LANES)
        bd = (lane[:, None] // group == lane[None, :] // group).astype(BF16)
        in_specs += [pl.BlockSpec((1, LANES), lambda b, i, j: (0, 0)),
                     pl.BlockSpec((LANES, LANES), lambda b, i, j: (0, 0))]
        args += [jnp.tile(gain.astype(F32), LANES // group).reshape(1, LANES), bd]
    rope_q = 0
    if rope is not None:
        rope_q, cos, sin_a, sin_b = rope
        in_specs += [pl.BlockSpec((tm, LANES), lambda b, i, j: (i, 0))] * 3
        args += [cos, sin_a, sin_b]
    out_specs = [pl.BlockSpec((1, tm, cw), lambda b, i, j: (b, i, j))]
    out_shape = [jax.ShapeDtypeStruct((g, t, width), BF16)]
    if want_f32:
        out_specs.append(pl.BlockSpec((1, tm, cw), lambda b, i, j: (b, i, j)))
        out_shape.append(jax.ShapeDtypeStruct((g, t, width), F32))
    outs = pl.pallas_call(
        functools.partial(_prep_kernel, norm=gain is not None, rope_q=rope_q, prescale=prescale,
                          postscale=postscale, want_f32=want_f32, group=group),
        grid=(g, t // tm, width // cw),
        in_specs=in_specs,
        out_specs=out_specs,
        out_shape=out_shape,
        compiler_params=_params("parallel", "parallel", "parallel"),
        name="head_prep",
    )(*args)
    return outs if want_f32 else outs[0]


def rope_tables(n_tokens, dim):
    rows = n_tokens // GRID_W
    row = jnp.repeat(jnp.arange(rows, dtype=F32), GRID_W)
    col = jnp.tile(jnp.arange(GRID_W, dtype=F32), rows)
    n_freq = dim // 4
    inv = ROPE_BASE ** (-jnp.arange(n_freq, dtype=F32) / n_freq)
    ar = row[:, None] * inv
    ac = col[:, None] * inv
    ang = jnp.concatenate([ar, ar, ac, ac], axis=-1)
    cos, sin = jnp.cos(ang), jnp.sin(ang)
    even = ((jnp.arange(dim) // n_freq) % 2 == 0)[None, :]
    sin_a = jnp.where(even, -sin, 0.0)
    sin_b = jnp.where(even, 0.0, sin)
    rep = LANES // dim
    return n_freq, jnp.tile(cos, (1, rep)), jnp.tile(sin_a, (1, rep)), jnp.tile(sin_b, (1, rep))


ATTN_ROWS = 128
ATTN_KEYS = 256


def _attn_kernel(q_ref, k_ref, v_ref, lp_ref, g_ref, o_ref, p_ref, *, lam_init, dh):
    lk = k_ref.shape[1]
    tq = min(q_ref.shape[1], ATTN_ROWS)
    tk = _tile(lk, ATTN_KEYS, LANES)
    lp = lp_ref[...]
    lam = (jnp.exp(jnp.sum(lp[0:1] * lp[1:2], axis=1, keepdims=True))
           - jnp.exp(jnp.sum(lp[2:3] * lp[3:4], axis=1, keepdims=True)) + lam_init)
    dim1 = (((1,), (1,)), ((), ()))
    lane_tiles = [slice(j, j + LANES) for j in range(0, tk, LANES)]
    tile = 0
    for hh in range(q_ref.shape[2] // LANES):
        cols = slice(hh * LANES, (hh + 1) * LANES)
        for q0 in range(0, q_ref.shape[1], tq):
            pb_ref = p_ref.at[tile % 2]
            tile += 1
            q = q_ref[0, q0:q0 + tq, cols]
            lane = lax.broadcasted_iota(jnp.int32, q.shape, 1)
            zero = jnp.zeros_like(q)
            qs = jnp.concatenate([jnp.where(lane < dh, q, zero), jnp.where(lane >= dh, q, zero)], axis=0)
            mx = jnp.full((2 * tq, LANES), -jnp.inf, F32)
            for k0 in range(0, lk, tk):
                s = lax.dot_general(qs, k_ref[0, k0:k0 + tk, cols], dim1, preferred_element_type=F32)
                for lt in lane_tiles:
                    mx = jnp.maximum(mx, s[:, lt])
            m = jnp.broadcast_to(jnp.max(mx, axis=1, keepdims=True), (2 * tq, LANES))
            ls = jnp.zeros((2 * tq, LANES), F32)
            for k0 in range(0, lk, tk):
                s = lax.dot_general(qs, k_ref[0, k0:k0 + tk, cols], dim1, preferred_element_type=F32)
                for lt in lane_tiles:
                    p = jnp.exp2(s[:, lt] - m)
                    ls = ls + p
                    pb_ref[:, k0 + lt.start:k0 + lt.stop] = p.astype(BF16)
            l = jnp.sum(ls, axis=1, keepdims=True)
            on = jnp.dot(pb_ref[...], v_ref[0, :, cols], preferred_element_type=F32) / l
            o = on[0:tq] - lam * on[tq:2 * tq]
            ms = jnp.mean(o * o, axis=-1, keepdims=True)
            o = o * lax.rsqrt(ms + EPS) * g_ref[...] * (1.0 - lam_init)
            o_ref[0, q0:q0 + tq, cols] = o.astype(o_ref.dtype)


def diff_attention(q, k, v, lam_params, subln_g, lam_init, dh):
    b, lq, w = q.shape
    lk = k.shape[1]
    h = w // LANES
    tq = _tile(lq, 2 * ATTN_ROWS, SUBLANES)
    hb = _tile(h, 4, 1) if lk <= 2 * ATTN_KEYS else 1
    wb = hb * LANES
    return pl.pallas_call(
        functools.partial(_attn_kernel, lam_init=lam_init, dh=dh),
        grid=(b, h // hb, lq // tq),
        in_specs=[pl.BlockSpec((1, tq, wb), lambda bi, hi, qi: (bi, qi, hi)),
                  pl.BlockSpec((1, lk, wb), lambda bi, hi, qi: (bi, 0, hi)),
                  pl.BlockSpec((1, lk, wb), lambda bi, hi, qi: (bi, 0, hi)),
                  pl.BlockSpec((4, dh), lambda bi, hi, qi: (0, 0)),
                  pl.BlockSpec((1, LANES), lambda bi, hi, qi: (0, 0))],
        out_specs=pl.BlockSpec((1, tq, wb), lambda bi, hi, qi: (bi, qi, hi)),
        out_shape=jax.ShapeDtypeStruct((b, lq, w), BF16),
        scratch_shapes=[pltpu.VMEM((2, 2 * min(tq, ATTN_ROWS), lk), BF16)],
        compiler_params=_params("parallel", "parallel", "parallel"),
        name="diff_attention",
    )(q, k, v, lam_params, subln_g.astype(F32).reshape(1, LANES))


def _retention_kernel(q_ref, k_ref, v_ref, gate_ref, dx_ref, s0f_ref, s0b_ref,
                      o_ref, sf_ref, sb_ref, acc_ref, *, chunk):
    seq = q_ref.shape[1]
    n_chunks = seq // chunk
    dx = dx_ref[0]
    lg = jnp.log1p(-jnp.exp2(-dx))
    lgf, lgb = lg[0:1], lg[1:2]
    ii = lax.broadcasted_iota(jnp.int32, (chunk, chunk), 0)
    jj = lax.broadcasted_iota(jnp.int32, (chunk, chunk), 1)
    d = (ii - jj).astype(F32)
    dmat = (jnp.where(d >= 0, jnp.exp(jnp.maximum(d, 0.0) * lgf[:, 0:1]), 0.0)
            + jnp.where(d <= 0, jnp.exp(jnp.maximum(-d, 0.0) * lgb[:, 0:1]), 0.0))
    pos = lax.broadcasted_iota(jnp.int32, (chunk, LANES), 0).astype(F32)
    qdec_f = jnp.exp((pos + 1.0) * lgf)
    kdec_f = jnp.exp((chunk - 1.0 - pos) * lgf)
    cdec_f = jnp.exp(chunk * lgf)
    qdec_b = jnp.exp((chunk - pos) * lgb)
    kdec_b = jnp.exp(pos * lgb)
    cdec_b = jnp.exp(chunk * lgb)
    dim0 = (((0,), (0,)), ((), ()))
    dim1 = (((1,), (1,)), ((), ()))

    def fwd(n, s):
        rows = pl.ds(pl.multiple_of(n * chunk, chunk), chunk)
        q, k, v = q_ref[0, rows, :], k_ref[0, rows, :], v_ref[0, rows, :]
        att = lax.dot_general(q, k, dim1, preferred_element_type=F32) * dmat
        acc_ref[rows, :] = (jnp.dot(att.astype(BF16), v, preferred_element_type=F32)
                            + jnp.dot((q * qdec_f).astype(BF16), s.astype(BF16),
                                      preferred_element_type=F32))
        return s * cdec_f + lax.dot_general((k * kdec_f).astype(BF16), v, dim0,
                                            preferred_element_type=F32)

    sf_ref[0, 0] = lax.fori_loop(0, n_chunks, fwd, s0f_ref[0, 0])

    def bwd(m, s):
        n = n_chunks - 1 - m
        rows = pl.ds(pl.multiple_of(n * chunk, chunk), chunk)
        q, k, v = q_ref[0, rows, :], k_ref[0, rows, :], v_ref[0, rows, :]
        acc_ref[rows, :] += jnp.dot((q * qdec_b).astype(BF16), s.astype(BF16),
                                    preferred_element_type=F32)
        return s * cdec_b + lax.dot_general((k * kdec_b).astype(BF16), v, dim0,
                                            preferred_element_type=F32)

    sb_ref[0, 0] = lax.fori_loop(0, n_chunks, bwd, s0b_ref[0, 0])

    o = acc_ref[...]
    ms = jnp.mean(o * o, axis=-1, keepdims=True)
    gate = gate_ref[0].astype(F32)
    o_ref[0] = (o * lax.rsqrt(ms + EPS) * (gate * jax.nn.sigmoid(gate))).astype(o_ref.dtype)


def bi_retention(q, k, v, proj, gate_col, dexp, s0f, s0b):
    b, seq, w = q.shape
    h = w // LANES
    chunk = _tile(seq, RET_CHUNK, LANES)
    gc = gate_col // LANES
    head = lambda bi, hi: (bi, 0, hi)
    state = lambda bi, hi: (bi, hi, 0, 0)
    return pl.pallas_call(
        functools.partial(_retention_kernel, chunk=chunk),
        grid=(b, h),
        in_specs=[pl.BlockSpec((1, seq, LANES), head),
                  pl.BlockSpec((1, seq, LANES), head),
                  pl.BlockSpec((1, seq, LANES), head),
                  pl.BlockSpec((1, seq, LANES), lambda bi, hi: (bi, 0, gc + hi)),
                  pl.BlockSpec((1, 2, LANES), lambda bi, hi: (hi, 0, 0)),
                  pl.BlockSpec((1, 1, LANES, LANES), state),
                  pl.BlockSpec((1, 1, LANES, LANES), state)],
        out_specs=[pl.BlockSpec((1, seq, LANES), head),
                   pl.BlockSpec((1, 1, LANES, LANES), state),
                   pl.BlockSpec((1, 1, LANES, LANES), state)],
        out_shape=[jax.ShapeDtypeStruct((b, seq, w), BF16),
                   jax.ShapeDtypeStruct((b, h, LANES, LANES), F32),
                   jax.ShapeDtypeStruct((b, h, LANES, LANES), F32)],
        scratch_shapes=[pltpu.VMEM((seq, LANES), F32)],
        compiler_params=_params("parallel", "parallel"),
        name="bi_retention",
    )(q, k, v, proj, dexp, s0f, s0b)


def _conv_ln_kernel(u_ref, prev_ref, next_ref, w_ref, g_ref, b_ref, o_ref, ext_ref, y_ref, shift_ref, *,
                    seq_len, taps, row_block):
    i = pl.program_id(1)
    c = pl.program_id(2)
    tm, cw = u_ref.shape[1], u_ref.shape[2]
    first = (i * tm) % seq_len == 0
    last = ((i + 1) * tm) % seq_len == 0
    zero = jnp.zeros((CONV_HALO, cw), F32)
    ext_ref[0:CONV_HALO, :] = jnp.where(first, zero, prev_ref[0])
    ext_ref[CONV_HALO:CONV_HALO + tm, :] = u_ref[0]
    ext_ref[CONV_HALO + tm:, :] = jnp.where(last, zero, next_ref[0])
    half = (taps - 1) // 2
    for r0 in range(0, tm, row_block):
        acc = None
        for b in range(SUBLANES):
            part = None
            for a in range(-((half + b) // SUBLANES), (half - b) // SUBLANES + 1):
                o = b + SUBLANES * a
                term = (ext_ref[pl.ds(CONV_HALO + r0 + SUBLANES * a, row_block + SUBLANES), :]
                        * w_ref[half + o:half + o + 1, :])
                part = term if part is None else part + term
            if b == 0:
                shifted = part[0:row_block]
            else:
                shift_ref[b % 2] = part
                shifted = shift_ref[b % 2, pl.ds(b, row_block), :]
            acc = shifted if acc is None else acc + shifted
        y_ref[c, pl.ds(r0, row_block), :] = acc

    @pl.when(c == pl.num_programs(2) - 1)
    def _():
        nc = y_ref.shape[0]
        d = nc * cw
        mu = sum(jnp.sum(y_ref[cc], axis=-1, keepdims=True) for cc in range(nc)) * (1.0 / d)
        var = sum(jnp.sum(jnp.square(y_ref[cc] - mu), axis=-1, keepdims=True)
                  for cc in range(nc)) * (1.0 / d)
        inv = lax.rsqrt(var + EPS)
        for cc in range(nc):
            cols = slice(cc * cw, (cc + 1) * cw)
            z = (y_ref[cc] - mu) * inv * g_ref[:, cols] + b_ref[:, cols]
            o_ref[0, :, cols] = (z * jax.nn.sigmoid(z)).astype(o_ref.dtype)


def conv_ln_silu(u, dw, ln_g, ln_b, seq_len):
    g, t, d = u.shape
    taps = dw.shape[0]
    half = (taps - 1) // 2
    assert taps % 2 == 1 and SUBLANES * (-(-half // SUBLANES)) <= CONV_HALO
    assert SUBLANES * (half // SUBLANES + 1) <= CONV_HALO
    tm = _tile(min(t, seq_len), 256, CONV_HALO)
    cw = _tile(d, 512, LANES)
    nh = tm // CONV_HALO
    n_halo_blocks = t // CONV_HALO
    w = jnp.zeros((2 * CONV_HALO, d), F32).at[:taps].set(dw.astype(F32))
    return pl.pallas_call(
        functools.partial(_conv_ln_kernel, seq_len=seq_len, taps=taps, row_block=min(tm, 32)),
        grid=(g, t // tm, d // cw),
        in_specs=[pl.BlockSpec((1, tm, cw), lambda b, i, c: (b, i, c)),
                  pl.BlockSpec((1, CONV_HALO, cw), lambda b, i, c: (b, jnp.maximum(i * nh - 1, 0), c)),
                  pl.BlockSpec((1, CONV_HALO, cw),
                               lambda b, i, c: (b, jnp.minimum((i + 1) * nh, n_halo_blocks - 1), c)),
                  pl.BlockSpec((2 * CONV_HALO, cw), lambda b, i, c: (0, c)),
                  pl.BlockSpec((1, d), lambda b, i, c: (0, 0)),
                  pl.BlockSpec((1, d), lambda b, i, c: (0, 0))],
        out_specs=pl.BlockSpec((1, tm, d), lambda b, i, c: (b, i, 0)),
        out_shape=jax.ShapeDtypeStruct((g, t, d), BF16),
        scratch_shapes=[pltpu.VMEM((tm + 2 * CONV_HALO, cw), F32),
                        pltpu.VMEM((d // cw, tm, cw), F32),
                        pltpu.VMEM((2, min(tm, 32) + SUBLANES, cw), F32)],
        compiler_params=_params("parallel", "parallel", "arbitrary"),
        name="conv_ln_silu",
    )(u, u, u, w, ln_g.astype(F32).reshape(1, d), ln_b.astype(F32).reshape(1, d))


FFN_HALO = 16


def _up_glu_kernel(h_ref, hp_ref, hn_ref, wg_ref, wv_ref, dg_ref, dv_ref, o_ref, hx_ref, *, seq_len):
    i = pl.program_id(1)
    tm = h_ref.shape[1]
    halo = FFN_HALO

    @pl.when(pl.program_id(2) == 0)
    def _():
        hx_ref[0:halo, :] = hp_ref[0]
        hx_ref[halo:halo + tm, :] = h_ref[0]
        hx_ref[halo + tm:, :] = hn_ref[0]

    cw = MXU_COLS if o_ref.shape[2] % MXU_COLS == 0 else LANES
    row = lax.broadcasted_iota(jnp.int32, (tm, cw), 0)
    tile_in_one_seq = seq_len % tm == 0
    if tile_in_one_seq:
        first = lax.rem(i * tm, seq_len) == 0
        last = lax.rem((i + 1) * tm, seq_len) == 0
    else:
        pos = lax.rem(i * tm + row, seq_len)
        no_prev = pos == 0
        no_next = pos == seq_len - 1

    def conv(w_ref, d_ref, cols):
        p = jnp.dot(hx_ref[...], w_ref[:, cols], preferred_element_type=F32)
        x = p[halo:halo + tm]
        before, after = p[halo - 1:halo], p[halo + tm:halo + tm + 1]
        if tile_in_one_seq:
            before = jnp.where(first, 0.0, before)
            after = jnp.where(last, 0.0, after)
        xm = jnp.where(row == 0, before, pltpu.roll(x, 1, axis=0))
        xp = jnp.where(row == tm - 1, after, pltpu.roll(x, tm - 1, axis=0))
        if not tile_in_one_seq:
            xm = jnp.where(no_prev, 0.0, xm)
            xp = jnp.where(no_next, 0.0, xp)
        d = d_ref[:, cols]
        return xm * d[0:1] + x * d[1:2] + xp * d[2:3]

    for c0 in range(0, o_ref.shape[2], cw):
        cols = slice(c0, c0 + cw)
        gate = conv(wg_ref, dg_ref, cols)
        val = conv(wv_ref, dv_ref, cols)
        o_ref[0, :, cols] = (gate * jax.nn.sigmoid(gate) * val).astype(o_ref.dtype)


def ffn_up_conv_glu(h, up, dw, seq_len):
    g, t, d = h.shape
    f = up.shape[1] // 2
    assert dw.shape[0] == 3
    tm = _tile(t, 1024, FFN_HALO)
    tf = _tile(f, 512, LANES)
    nj = f // tf
    nh = tm // FFN_HALO
    n_halo_blocks = t // FFN_HALO
    w = jnp.zeros((SUBLANES, 2 * f), F32).at[:3].set(dw.astype(F32))
    return pl.pallas_call(
        functools.partial(_up_glu_kernel, seq_len=seq_len),
        grid=(g, t // tm, nj),
        in_specs=[pl.BlockSpec((1, tm, d), lambda b, i, j: (b, i, 0), pipeline_mode=pl.Buffered(1)),
                  pl.BlockSpec((1, FFN_HALO, d), lambda b, i, j: (b, jnp.maximum(i * nh - 1, 0), 0)),
                  pl.BlockSpec((1, FFN_HALO, d),
                               lambda b, i, j: (b, jnp.minimum((i + 1) * nh, n_halo_blocks - 1), 0)),
                  pl.BlockSpec((d, tf), lambda b, i, j: (0, j)),
                  pl.BlockSpec((d, tf), lambda b, i, j: (0, j + nj)),
                  pl.BlockSpec((SUBLANES, tf), lambda b, i, j: (0, j)),
                  pl.BlockSpec((SUBLANES, tf), lambda b, i, j: (0, j + nj))],
        out_specs=pl.BlockSpec((1, tm, tf), lambda b, i, j: (b, i, j)),
        out_shape=jax.ShapeDtypeStruct((g, t, f), BF16),
        scratch_shapes=[pltpu.VMEM((tm + 2 * FFN_HALO, d), BF16)],
        compiler_params=_params("parallel", "parallel", "arbitrary"),
        name="ffn_up_conv_glu",
    )(h, h, h, up, up, w, w)


def _mods(mods_l, rows, d):
    m = mods_l[rows[0]:rows[1]]
    return [m[:, k * d:(k + 1) * d].reshape(-1, 1, d) for k in range(6)]


def _ffn_block(x, mods, norm_g, up, dw, down, seq_len):
    sh2, sc2, g2 = mods
    h = modulate(x, norm_g, sc2, sh2)
    a = ffn_up_conv_glu(h, up, dw, seq_len)
    return matmul_residual(a, down, x, g2)


def kernel(x_prompt, x_sample, cache_k_a, cache_v_a, state_ret_fwd, state_ret_bwd, c, c_ctx, adaln_w, adaln_b, norm1_g, norm2_g, w_in_ab, w_out_ab, q_norm_g, k_norm_g, lambda_q1, lambda_k1, lambda_q2, lambda_k2, subln_g, ret_decay_exp_fwd, ret_decay_exp_bwd, conv_pw1, conv_dw, conv_ln_g, conv_ln_b, conv_pw2, ffn_up, ffn_dw, ffn_down):
    batch, seq, d = x_prompt.shape
    dec_batch, dec_seq, _ = x_sample.shape
    depth = adaln_w.shape[0]
    _, _, past_len, h_a, dk_a = cache_k_a.shape
    dh_a = dk_a // 2
    dv_a = cache_v_a.shape[-1]
    _, _, h_b, dk_b, dv_b = state_ret_fwd.shape
    assert dk_a == LANES and dv_a == LANES and dk_b == LANES and dv_b == LANES
    assert dec_batch + 1 <= SUBLANES
    w_qa, w_va, w_qb, w_vb = h_a * dk_a, h_a * dv_a, h_b * dk_b, h_b * dv_b
    col = {}
    off = 0
    for name, width in (("qa", w_qa), ("ka", w_qa), ("va", w_va), ("qb", w_qb), ("kb", w_qb),
                        ("vb", w_vb), ("gb", w_vb)):
        col[name] = off
        off += width

    conds = jnp.zeros((SUBLANES, d), F32).at[0].set(c_ctx).at[1:1 + dec_batch].set(c)
    mods_all = adaln(conds, adaln_w, adaln_b)

    w_in_bf = w_in_ab.astype(BF16)
    w_out_bf = w_out_ab.astype(BF16)
    pw1_bf = conv_pw1.astype(BF16)
    pw2_bf = conv_pw2.astype(BF16)
    up_bf = ffn_up.astype(BF16)
    down_bf = ffn_down.astype(BF16)

    rope_a = rope_tables(dec_seq, dh_a)
    rope_b = rope_tables(dec_seq, dk_b)

    def run(x, rows, seq_len, latent):
        g, t, _ = x.shape
        n_seq = g * t // seq_len
        new_ctx = []
        for l in range(depth):
            sh1, sc1, g1, sh2, sc2, g2 = _mods(mods_all[l], rows, d)
            h = modulate(x, norm1_g[l], sc1, sh1)
            i = l // 2
            if l % 2 == 0:
                proj = matmul(h, w_in_bf[i], F32)
                ra = rope_a if latent else None
                rb = rope_b if latent else None
                qa = head_prep(proj, col["qa"], w_qa, gain=q_norm_g[i], group=dh_a, rope=ra,
                               postscale=dh_a ** -0.5 * math.log2(math.e))
                ka = head_prep(proj, col["ka"], w_qa, gain=k_norm_g[i], group=dh_a, rope=ra,
                               want_f32=not latent)
                va = head_prep(proj, col["va"], w_va, want_f32=not latent)
                qb = head_prep(proj, col["qb"], w_qb, rope=rb)
                kb = head_prep(proj, col["kb"], w_qb, rope=rb, prescale=dk_b ** -0.5)
                vb = head_prep(proj, col["vb"], w_vb)
                seqs = lambda z: z.reshape(n_seq, seq_len, z.shape[-1])
                if latent:
                    k_all = jnp.concatenate(
                        [cache_k_a[:, i].reshape(dec_batch, past_len, w_qa).astype(BF16), ka], axis=1)
                    v_all = jnp.concatenate(
                        [cache_v_a[:, i].reshape(dec_batch, past_len, w_va).astype(BF16), va], axis=1)
                    s0f, s0b = state_ret_fwd[:, i], state_ret_bwd[:, i]
                else:
                    ka, ka_f32 = ka
                    va, va_f32 = va
                    k_all, v_all = seqs(ka), seqs(va)
                    s0f = jnp.zeros((n_seq, h_b, dk_b, dv_b), F32)
                    s0b = s0f
                lam_init = 0.8 - 0.6 * math.exp(-0.3 * l)
                lam_params = jnp.stack([lambda_q1[i], lambda_k1[i], lambda_q2[i], lambda_k2[i]]).astype(F32)
                oa = diff_attention(seqs(qa), k_all, v_all, lam_params, subln_g[i], lam_init, dh_a)
                dexp = jnp.stack([ret_decay_exp_fwd[i], ret_decay_exp_bwd[i]], axis=1).astype(F32)
                dexp = jnp.broadcast_to(dexp[:, :, None], (h_b, 2, LANES))
                ob, s_f, s_b = bi_retention(seqs(qb), seqs(kb), seqs(vb), seqs(proj), col["gb"], dexp,
                                            s0f, s0b)
                mix = jnp.concatenate([oa.reshape(g, t, w_va), ob.reshape(g, t, w_vb)], axis=-1)
                x = matmul_residual(mix, w_out_bf[i], x, g1)
                if not latent:
                    new_ctx.append((ka_f32.reshape(n_seq, seq_len, h_a, dk_a),
                                    va_f32.reshape(n_seq, seq_len, h_a, dv_a), s_f, s_b))
            else:
                u = matmul_glu(h, pw1_bf[i], F32)
                z = conv_ln_silu(u, conv_dw[i], conv_ln_g[i], conv_ln_b[i], seq_len)
                x = matmul_residual(z, pw2_bf[i], x, g1)
            x = _ffn_block(x, (sh2, sc2, g2), norm2_g[l], up_bf[l], ffn_dw[l], down_bf[l], seq_len)
        return x, new_ctx

    y_ctx, new_ctx = run(x_prompt.reshape(1, batch * seq, d), (0, 1), seq, False)
    y_lat, _ = run(x_sample, (1, 1 + dec_batch), dec_seq, True)
    stack = lambda k: jnp.stack([n[k] for n in new_ctx], axis=1)
    return (y_ctx.reshape(batch, seq, d), y_lat, stack(0), stack(1), stack(2), stack(3))
```

```python
import functools
import math

import jax
import jax.numpy as jnp
from jax import lax
from jax.experimental import pallas as pl
from jax.experimental.pallas import tpu as pltpu

F32 = jnp.float32
BF16 = jnp.bfloat16
EPS = 1e-6
GRID_W = 64
ROPE_BASE = 10000.0
LANES = 128
SUBLANES = 8
MXU_COLS = 256
CONV_HALO = 16
VMEM_LIMIT_V7X = 56 * 1024 * 1024
RET_CHUNK = 256


def _tile(dim, pref, align):
    if dim <= pref:
        return dim
    t = (pref // align) * align
    while t >= align:
        if dim % t == 0:
            return t
        t -= align
    return dim


def _params(*sem):
    return pltpu.CompilerParams(dimension_semantics=sem, vmem_limit_bytes=VMEM_LIMIT_V7X)


def _adaln_kernel(c_ref, w_ref, b_ref, o_ref):
    c = c_ref[...]
    s = (c * jax.nn.sigmoid(c)).astype(BF16)
    w = w_ref[0].astype(BF16)
    o_ref[0] = jnp.dot(s, w, preferred_element_type=F32) + b_ref[0]


def adaln(conds, w, b):
    depth, d, n = w.shape
    tn = _tile(n, 512, LANES)
    return pl.pallas_call(
        _adaln_kernel,
        grid=(depth, n // tn),
        in_specs=[pl.BlockSpec((SUBLANES, d), lambda l, j: (0, 0)),
                  pl.BlockSpec((1, d, tn), lambda l, j: (l, 0, j)),
                  pl.BlockSpec((1, 1, tn), lambda l, j: (l, 0, j))],
        out_specs=pl.BlockSpec((1, SUBLANES, tn), lambda l, j: (l, 0, j)),
        out_shape=jax.ShapeDtypeStruct((depth, SUBLANES, n), F32),
        compiler_params=_params("parallel", "parallel"),
        name="adaln",
    )(conds, w, b.reshape(depth, 1, n))


def _modulate_kernel(x_ref, g_ref, sc_ref, sh_ref, o_ref):
    d = x_ref.shape[2]
    cw = _tile(d, 512, LANES)
    chunks = [slice(c0, c0 + cw) for c0 in range(0, d, cw)]
    ss = sum(jnp.sum(jnp.square(x_ref[0, :, cs]), axis=-1, keepdims=True) for cs in chunks)
    inv = lax.rsqrt(ss * (1.0 / d) + EPS)
    for cs in chunks:
        y = x_ref[0, :, cs] * inv * g_ref[:, cs]
        o_ref[0, :, cs] = (y * (1.0 + sc_ref[0, :, cs]) + sh_ref[0, :, cs]).astype(o_ref.dtype)


def modulate(x, gain, scale, shift):
    g, t, d = x.shape
    tm = _tile(t, 256, SUBLANES)
    return pl.pallas_call(
        _modulate_kernel,
        grid=(g, t // tm),
        in_specs=[pl.BlockSpec((1, tm, d), lambda b, i: (b, i, 0)),
                  pl.BlockSpec((1, d), lambda b, i: (0, 0)),
                  pl.BlockSpec((1, 1, d), lambda b, i: (b, 0, 0)),
                  pl.BlockSpec((1, 1, d), lambda b, i: (b, 0, 0))],
        out_specs=pl.BlockSpec((1, tm, d), lambda b, i: (b, i, 0)),
        out_shape=jax.ShapeDtypeStruct((g, t, d), BF16),
        compiler_params=_params("parallel", "parallel"),
        name="modulate",
    )(x, gain.reshape(1, d), scale, shift)


def _mm_kernel(a_ref, w_ref, o_ref):
    o_ref[0] = jnp.dot(a_ref[0], w_ref[...], preferred_element_type=F32).astype(o_ref.dtype)


def _mm_res_kernel(a_ref, w_ref, r_ref, g_ref, o_ref):
    acc = jnp.dot(a_ref[0], w_ref[...], preferred_element_type=F32)
    o_ref[0] = r_ref[0] + g_ref[0] * acc


def _mm_glu_kernel(a_ref, wa_ref, wg_ref, o_ref):
    a = jnp.dot(a_ref[0], wa_ref[...], preferred_element_type=F32)
    g = jnp.dot(a_ref[0], wg_ref[...], preferred_element_type=F32)
    o_ref[0] = (a * jax.nn.sigmoid(g)).astype(o_ref.dtype)


def _mm_tiles(t, k, n):
    tm = _tile(t, 1024, SUBLANES)
    tn = _tile(n, 512 if k <= 8192 else 256, LANES)
    return tm, tn


def _lhs_spec(tm, k):
    mode = pl.Buffered(1) if k > 8192 else None
    return pl.BlockSpec((1, tm, k), lambda b, i, j: (b, i, 0), pipeline_mode=mode)


def matmul(a, w, out_dtype):
    g, t, k = a.shape
    n = w.shape[1]
    tm, tn = _mm_tiles(t, k, n)
    return pl.pallas_call(
        _mm_kernel,
        grid=(g, t // tm, n // tn),
        in_specs=[pl.BlockSpec((1, tm, k), lambda b, i, j: (b, i, 0)),
                  pl.BlockSpec((k, tn), lambda b, i, j: (0, j))],
        out_specs=pl.BlockSpec((1, tm, tn), lambda b, i, j: (b, i, j)),
        out_shape=jax.ShapeDtypeStruct((g, t, n), out_dtype),
        compiler_params=_params("parallel", "parallel", "parallel"),
        name="matmul",
    )(a, w)


def matmul_residual(a, w, res, gate):
    g, t, k = a.shape
    n = w.shape[1]
    tm, tn = _mm_tiles(t, k, n)
    return pl.pallas_call(
        _mm_res_kernel,
        grid=(g, t // tm, n // tn),
        in_specs=[_lhs_spec(tm, k),
                  pl.BlockSpec((k, tn), lambda b, i, j: (0, j)),
                  pl.BlockSpec((1, tm, tn), lambda b, i, j: (b, i, j)),
                  pl.BlockSpec((1, 1, tn), lambda b, i, j: (b, 0, j))],
        out_specs=pl.BlockSpec((1, tm, tn), lambda b, i, j: (b, i, j)),
        out_shape=jax.ShapeDtypeStruct((g, t, n), F32),
        compiler_params=_params("parallel", "parallel", "parallel"),
        name="matmul_residual",
    )(a, w, res, gate)


def matmul_glu(a, w, out_dtype):
    g, t, k = a.shape
    n = w.shape[1] // 2
    tm, tn = _mm_tiles(t, k, n)
    nj = n // tn
    return pl.pallas_call(
        _mm_glu_kernel,
        grid=(g, t // tm, nj),
        in_specs=[pl.BlockSpec((1, tm, k), lambda b, i, j: (b, i, 0)),
                  pl.BlockSpec((k, tn), lambda b, i, j: (0, j)),
                  pl.BlockSpec((k, tn), lambda b, i, j: (0, j + nj))],
        out_specs=pl.BlockSpec((1, tm, tn), lambda b, i, j: (b, i, j)),
        out_shape=jax.ShapeDtypeStruct((g, t, n), out_dtype),
        compiler_params=_params("parallel", "parallel", "parallel"),
        name="matmul_glu",
    )(a, w, w)


def _prep_kernel(*refs, norm, rope_q, prescale, postscale, want_f32, group):
    refs = list(refs)
    x_ref = refs.pop(0)
    if norm:
        g_ref = refs.pop(0)
        bd_ref = refs.pop(0)
    if rope_q:
        cos_ref = refs.pop(0)
        sa_ref = refs.pop(0)
        sb_ref = refs.pop(0)
    o_ref = refs.pop(0)
    f_ref = refs.pop(0) if want_f32 else None
    for hh in range(x_ref.shape[2] // LANES):
        cols = slice(hh * LANES, (hh + 1) * LANES)
        y = x_ref[0, :, cols].astype(F32)
        if norm:
            sq = y * y
            hi = sq.astype(BF16)
            lo = (sq - hi.astype(F32)).astype(BF16)
            ss = (jnp.dot(hi, bd_ref[...], preferred_element_type=F32)
                  + jnp.dot(lo, bd_ref[...], preferred_element_type=F32))
            y = y * lax.rsqrt(ss * (1.0 / group) + EPS) * g_ref[...]
        if prescale != 1.0:
            y = y * prescale
        if rope_q:
            y = (y * cos_ref[...]
                 + pltpu.roll(y, LANES - rope_q, axis=1) * sa_ref[...]
                 + pltpu.roll(y, rope_q, axis=1) * sb_ref[...])
        if want_f32:
            f_ref[0, :, cols] = y
        if postscale != 1.0:
            y = y * postscale
        o_ref[0, :, cols] = y.astype(o_ref.dtype)


def head_prep(proj, col_start, width, *, gain=None, group=None, rope=None, prescale=1.0,
              postscale=1.0, want_f32=False):
    g, t, _ = proj.shape
    cw = _tile(width, 1024, LANES)
    tm = _tile(t, 512, SUBLANES)
    c0 = col_start // cw
    assert col_start % cw == 0
    in_specs = [pl.BlockSpec((1, tm, cw), lambda b, i, j: (b, i, c0 + j))]
    args = [proj]
    if gain is not None:
        lane = jnp.arange(LANES)
        bd = (lane[:, None] // group == lane[None, :] // group).astype(BF16)
        in_specs += [pl.BlockSpec((1, LANES), lambda b, i, j: (0, 0)),
                     pl.BlockSpec((LANES, LANES), lambda b, i, j: (0, 0))]
        args += [jnp.tile(gain.astype(F32), LANES // group).reshape(1, LANES), bd]
    rope_q = 0
    if rope is not None:
        rope_q, cos, sin_a, sin_b = rope
        in_specs += [pl.BlockSpec((tm, LANES), lambda b, i, j: (i, 0))] * 3
        args += [cos, sin_a, sin_b]
    out_specs = [pl.BlockSpec((1, tm, cw), lambda b, i, j: (b, i, j))]
    out_shape = [jax.ShapeDtypeStruct((g, t, width), BF16)]
    if want_f32:
        out_specs.append(pl.BlockSpec((1, tm, cw), lambda b, i, j: (b, i, j)))
        out_shape.append(jax.ShapeDtypeStruct((g, t, width), F32))
    outs = pl.pallas_call(
        functools.partial(_prep_kernel, norm=gain is not None, rope_q=rope_q, prescale=prescale,
                          postscale=postscale, want_f32=want_f32, group=group),
        grid=(g, t // tm, width // cw),
        in_specs=in_specs,
        out_specs=out_specs,
        out_shape=out_shape,
        compiler_params=_params("parallel", "parallel", "parallel"),
        name="head_prep",
    )(*args)
    return outs if want_f32 else outs[0]


def rope_tables(n_tokens, dim):
    rows = n_tokens // GRID_W
    row = jnp.repeat(jnp.arange(rows, dtype=F32), GRID_W)
    col = jnp.tile(jnp.arange(GRID_W, dtype=F32), rows)
    n_freq = dim // 4
    inv = ROPE_BASE ** (-jnp.arange(n_freq, dtype=F32) / n_freq)
    ar = row[:, None] * inv
    ac = col[:, None] * inv
    ang = jnp.concatenate([ar, ar, ac, ac], axis=-1)
    cos, sin = jnp.cos(ang), jnp.sin(ang)
    even = ((jnp.arange(dim) // n_freq) % 2 == 0)[None, :]
    sin_a = jnp.where(even, -sin, 0.0)
    sin_b = jnp.where(even, 0.0, sin)
    rep = LANES // dim
    return n_freq, jnp.tile(cos, (1, rep)), jnp.tile(sin_a, (1, rep)), jnp.tile(sin_b, (1, rep))


ATTN_ROWS = 128
ATTN_KEYS = 256


def _attn_kernel(q_ref, k_ref, v_ref, lp_ref, g_ref, o_ref, p_ref, s_ref, *, lam_init, dh):
    lk = k_ref.shape[1]
    tq = min(q_ref.shape[1], ATTN_ROWS)
    tk = _tile(lk, ATTN_KEYS, LANES)
    lp = lp_ref[...]
    lam = (jnp.exp(jnp.sum(lp[0:1] * lp[1:2], axis=1, keepdims=True))
           - jnp.exp(jnp.sum(lp[2:3] * lp[3:4], axis=1, keepdims=True)) + lam_init)
    dim1 = (((1,), (1,)), ((), ()))
    lane_tiles = [slice(j, j + LANES) for j in range(0, tk, LANES)]
    tile = 0
    for hh in range(q_ref.shape[2] // LANES):
        cols = slice(hh * LANES, (hh + 1) * LANES)
        for q0 in range(0, q_ref.shape[1], tq):
            pb_ref = p_ref.at[tile % 2]
            sb_ref = s_ref.at[tile % 2]
            tile += 1
            q = q_ref[0, q0:q0 + tq, cols]
            lane = lax.broadcasted_iota(jnp.int32, q.shape, 1)
            zero = jnp.zeros_like(q)
            qs = jnp.concatenate([jnp.where(lane < dh, q, zero), jnp.where(lane >= dh, q, zero)], axis=0)
            mx = jnp.full((2 * tq, LANES), -jnp.inf, F32)
            for k0 in range(0, lk, tk):
                s = lax.dot_general(qs, k_ref[0, k0:k0 + tk, cols], dim1, preferred_element_type=F32)
                sb_ref[:, k0:k0 + tk] = s
                for lt in lane_tiles:
                    mx = jnp.maximum(mx, s[:, lt])
            m = jnp.broadcast_to(jnp.max(mx, axis=1, keepdims=True), (2 * tq, LANES))
            ls = jnp.zeros((2 * tq, LANES), F32)
            for k0 in range(0, lk, LANES):
                p = jnp.exp2(sb_ref[:, k0:k0 + LANES] - m)
                ls = ls + p
                pb_ref[:, k0:k0 + LANES] = p.astype(BF16)
            l = jnp.sum(ls, axis=1, keepdims=True)
            on = jnp.dot(pb_ref[...], v_ref[0, :, cols], preferred_element_type=F32) / l
            o = on[0:tq] - lam * on[tq:2 * tq]
            ms = jnp.mean(o * o, axis=-1, keepdims=True)
            o = o * lax.rsqrt(ms + EPS) * g_ref[...] * (1.0 - lam_init)
            o_ref[0, q0:q0 + tq, cols] = o.astype(o_ref.dtype)


def diff_attention(q, k, v, lam_params, subln_g, lam_init, dh):
    b, lq, w = q.shape
    lk = k.shape[1]
    h = w // LANES
    tq = _tile(lq, 4 * ATTN_ROWS, SUBLANES)
    hb = _tile(h, 4, 1) if lk <= 2 * ATTN_KEYS else 1
    wb = hb * LANES
    return pl.pallas_call(
        functools.partial(_attn_kernel, lam_init=lam_init, dh=dh),
        grid=(b, h // hb, lq // tq),
        in_specs=[pl.BlockSpec((1, tq, wb), lambda bi, hi, qi: (bi, qi, hi)),
                  pl.BlockSpec((1, lk, wb), lambda bi, hi, qi: (bi, 0, hi)),
                  pl.BlockSpec((1, lk, wb), lambda bi, hi, qi: (bi, 0, hi)),
                  pl.BlockSpec((4, dh), lambda bi, hi, qi: (0, 0)),
                  pl.BlockSpec((1, LANES), lambda bi, hi, qi: (0, 0))],
        out_specs=pl.BlockSpec((1, tq, wb), lambda bi, hi, qi: (bi, qi, hi)),
        out_shape=jax.ShapeDtypeStruct((b, lq, w), BF16),
        scratch_shapes=[pltpu.VMEM((2, 2 * min(tq, ATTN_ROWS), lk), BF16),
                        pltpu.VMEM((2, 2 * min(tq, ATTN_ROWS), lk), F32)],
        compiler_params=_params("parallel", "parallel", "parallel"),
        name="diff_attention",
    )(q, k, v, lam_params, subln_g.astype(F32).reshape(1, LANES))


def _retention_kernel(q_ref, k_ref, v_ref, gate_ref, dx_ref, s0f_ref, s0b_ref,
                      o_ref, sf_ref, sb_ref, acc_ref, accb_ref, *, chunk):
    seq = q_ref.shape[1]
    n_chunks = seq // chunk
    dx = dx_ref[0]
    lg = jnp.log1p(-jnp.exp2(-dx))
    lgf, lgb = lg[0:1], lg[1:2]
    ii = lax.broadcasted_iota(jnp.int32, (chunk, chunk), 0)
    jj = lax.broadcasted_iota(jnp.int32, (chunk, chunk), 1)
    d = (ii - jj).astype(F32)
    dmat = (jnp.where(d >= 0, jnp.exp(jnp.maximum(d, 0.0) * lgf[:, 0:1]), 0.0)
            + jnp.where(d <= 0, jnp.exp(jnp.maximum(-d, 0.0) * lgb[:, 0:1]), 0.0))
    pos = lax.broadcasted_iota(jnp.int32, (chunk, LANES), 0).astype(F32)
    qdec_f = jnp.exp((pos + 1.0) * lgf)
    kdec_f = jnp.exp((chunk - 1.0 - pos) * lgf)
    cdec_f = jnp.exp(chunk * lgf)
    qdec_b = jnp.exp((chunk - pos) * lgb)
    kdec_b = jnp.exp(pos * lgb)
    cdec_b = jnp.exp(chunk * lgb)
    dim0 = (((0,), (0,)), ((), ()))
    dim1 = (((1,), (1,)), ((), ()))

    def step(n, carry):
        s_f, s_b = carry
        rows = pl.ds(pl.multiple_of(n * chunk, chunk), chunk)
        q, k, v = q_ref[0, rows, :], k_ref[0, rows, :], v_ref[0, rows, :]
        att = lax.dot_general(q, k, dim1, preferred_element_type=F32) * dmat
        acc_ref[rows, :] = (jnp.dot(att.astype(BF16), v, preferred_element_type=F32)
                            + jnp.dot((q * qdec_f).astype(BF16), s_f.astype(BF16),
                                      preferred_element_type=F32))
        s_f = s_f * cdec_f + lax.dot_general((k * kdec_f).astype(BF16), v, dim0,
                                             preferred_element_type=F32)
        rows = pl.ds(pl.multiple_of((n_chunks - 1 - n) * chunk, chunk), chunk)
        q, k, v = q_ref[0, rows, :], k_ref[0, rows, :], v_ref[0, rows, :]
        accb_ref[rows, :] = jnp.dot((q * qdec_b).astype(BF16), s_b.astype(BF16),
                                    preferred_element_type=F32)
        s_b = s_b * cdec_b + lax.dot_general((k * kdec_b).astype(BF16), v, dim0,
                                             preferred_element_type=F32)
        return s_f, s_b

    sf_ref[0, 0], sb_ref[0, 0] = lax.fori_loop(0, n_chunks, step, (s0f_ref[0, 0], s0b_ref[0, 0]))

    o = acc_ref[...] + accb_ref[...]
    ms = jnp.mean(o * o, axis=-1, keepdims=True)
    gate = gate_ref[0].astype(F32)
    o_ref[0] = (o * lax.rsqrt(ms + EPS) * (gate * jax.nn.sigmoid(gate))).astype(o_ref.dtype)


def bi_retention(q, k, v, proj, gate_col, dexp, s0f, s0b):
    b, seq, w = q.shape
    h = w // LANES
    chunk = _tile(seq, RET_CHUNK, LANES)
    gc = gate_col // LANES
    head = lambda bi, hi: (bi, 0, hi)
    state = lambda bi, hi: (bi, hi, 0, 0)
    return pl.pallas_call(
        functools.partial(_retention_kernel, chunk=chunk),
        grid=(b, h),
        in_specs=[pl.BlockSpec((1, seq, LANES), head),
                  pl.BlockSpec((1, seq, LANES), head),
                  pl.BlockSpec((1, seq, LANES), head),
                  pl.BlockSpec((1, seq, LANES), lambda bi, hi: (bi, 0, gc + hi)),
                  pl.BlockSpec((1, 2, LANES), lambda bi, hi: (hi, 0, 0)),
                  pl.BlockSpec((1, 1, LANES, LANES), state),
                  pl.BlockSpec((1, 1, LANES, LANES), state)],
        out_specs=[pl.BlockSpec((1, seq, LANES), head),
                   pl.BlockSpec((1, 1, LANES, LANES), state),
                   pl.BlockSpec((1, 1, LANES, LANES), state)],
        out_shape=[jax.ShapeDtypeStruct((b, seq, w), BF16),
                   jax.ShapeDtypeStruct((b, h, LANES, LANES), F32),
                   jax.ShapeDtypeStruct((b, h, LANES, LANES), F32)],
        scratch_shapes=[pltpu.VMEM((seq, LANES), F32), pltpu.VMEM((seq, LANES), F32)],
        compiler_params=_params("parallel", "parallel"),
        name="bi_retention",
    )(q, k, v, proj, dexp, s0f, s0b)


def _conv_ln_kernel(u_ref, prev_ref, next_ref, w_ref, g_ref, b_ref, o_ref, ext_ref, y_ref, shift_ref, *,
                    seq_len, taps, row_block):
    i = pl.program_id(1)
    c = pl.program_id(2)
    tm, cw = u_ref.shape[1], u_ref.shape[2]
    first = (i * tm) % seq_len == 0
    last = ((i + 1) * tm) % seq_len == 0
    zero = jnp.zeros((CONV_HALO, cw), F32)
    ext_ref[0:CONV_HALO, :] = jnp.where(first, zero, prev_ref[0])
    ext_ref[CONV_HALO:CONV_HALO + tm, :] = u_ref[0]
    ext_ref[CONV_HALO + tm:, :] = jnp.where(last, zero, next_ref[0])
    half = (taps - 1) // 2
    for r0 in range(0, tm, row_block):
        acc = None
        for b in range(SUBLANES):
            part = None
            for a in range(-((half + b) // SUBLANES), (half - b) // SUBLANES + 1):
                o = b + SUBLANES * a
                term = (ext_ref[pl.ds(CONV_HALO + r0 + SUBLANES * a, row_block + SUBLANES), :]
                        * w_ref[half + o:half + o + 1, :])
                part = term if part is None else part + term
            if b == 0:
                shifted = part[0:row_block]
            else:
                shift_ref[b % 2] = part
                shifted = shift_ref[b % 2, pl.ds(b, row_block), :]
            acc = shifted if acc is None else acc + shifted
        y_ref[c, pl.ds(r0, row_block), :] = acc

    @pl.when(c == pl.num_programs(2) - 1)
    def _():
        nc = y_ref.shape[0]
        d = nc * cw
        mu = sum(jnp.sum(y_ref[cc], axis=-1, keepdims=True) for cc in range(nc)) * (1.0 / d)
        var = sum(jnp.sum(jnp.square(y_ref[cc] - mu), axis=-1, keepdims=True)
                  for cc in range(nc)) * (1.0 / d)
        inv = lax.rsqrt(var + EPS)
        for cc in range(nc):
            cols = slice(cc * cw, (cc + 1) * cw)
            z = (y_ref[cc] - mu) * inv * g_ref[:, cols] + b_ref[:, cols]
            o_ref[0, :, cols] = (z * jax.nn.sigmoid(z)).astype(o_ref.dtype)


def conv_ln_silu(u, dw, ln_g, ln_b, seq_len):
    g, t, d = u.shape
    taps = dw.shape[0]
    half = (taps - 1) // 2
    assert taps % 2 == 1 and SUBLANES * (-(-half // SUBLANES)) <= CONV_HALO
    assert SUBLANES * (half // SUBLANES + 1) <= CONV_HALO
    tm = _tile(min(t, seq_len), 256, CONV_HALO)
    cw = _tile(d, 512, LANES)
    nh = tm // CONV_HALO
    n_halo_blocks = t // CONV_HALO
    w = jnp.zeros((2 * CONV_HALO, d), F32).at[:taps].set(dw.astype(F32))
    return pl.pallas_call(
        functools.partial(_conv_ln_kernel, seq_len=seq_len, taps=taps, row_block=min(tm, 32)),
        grid=(g, t // tm, d // cw),
        in_specs=[pl.BlockSpec((1, tm, cw), lambda b, i, c: (b, i, c)),
                  pl.BlockSpec((1, CONV_HALO, cw), lambda b, i, c: (b, jnp.maximum(i * nh - 1, 0), c)),
                  pl.BlockSpec((1, CONV_HALO, cw),
                               lambda b, i, c: (b, jnp.minimum((i + 1) * nh, n_halo_blocks - 1), c)),
                  pl.BlockSpec((2 * CONV_HALO, cw), lambda b, i, c: (0, c)),
                  pl.BlockSpec((1, d), lambda b, i, c: (0, 0)),
                  pl.BlockSpec((1, d), lambda b, i, c: (0, 0))],
        out_specs=pl.BlockSpec((1, tm, d), lambda b, i, c: (b, i, 0)),
        out_shape=jax.ShapeDtypeStruct((g, t, d), BF16),
        scratch_shapes=[pltpu.VMEM((tm + 2 * CONV_HALO, cw), F32),
                        pltpu.VMEM((d // cw, tm, cw), F32),
                        pltpu.VMEM((2, min(tm, 32) + SUBLANES, cw), F32)],
        compiler_params=_params("parallel", "parallel", "arbitrary"),
        name="conv_ln_silu",
    )(u, u, u, w, ln_g.astype(F32).reshape(1, d), ln_b.astype(F32).reshape(1, d))


FFN_HALO = 16


def _up_glu_kernel(h_ref, hp_ref, hn_ref, wg_ref, wv_ref, dg_ref, dv_ref, o_ref, hx_ref, *, seq_len):
    i = pl.program_id(1)
    tm = h_ref.shape[1]
    halo = FFN_HALO

    @pl.when(pl.program_id(2) == 0)
    def _():
        hx_ref[0:halo, :] = hp_ref[0]
        hx_ref[halo:halo + tm, :] = h_ref[0]
        hx_ref[halo + tm:, :] = hn_ref[0]

    cw = MXU_COLS if o_ref.shape[2] % MXU_COLS == 0 else LANES
    row = lax.broadcasted_iota(jnp.int32, (tm, cw), 0)
    tile_in_one_seq = seq_len % tm == 0
    if tile_in_one_seq:
        first = lax.rem(i * tm, seq_len) == 0
        last = lax.rem((i + 1) * tm, seq_len) == 0
    else:
        pos = lax.rem(i * tm + row, seq_len)
        no_prev = pos == 0
        no_next = pos == seq_len - 1

    def conv(w_ref, d_ref, cols):
        p = jnp.dot(hx_ref[...], w_ref[:, cols], preferred_element_type=F32)
        x = p[halo:halo + tm]
        before, after = p[halo - 1:halo], p[halo + tm:halo + tm + 1]
        if tile_in_one_seq:
            before = jnp.where(first, 0.0, before)
            after = jnp.where(last, 0.0, after)
        xm = jnp.where(row == 0, before, pltpu.roll(x, 1, axis=0))
        xp = jnp.where(row == tm - 1, after, pltpu.roll(x, tm - 1, axis=0))
        if not tile_in_one_seq:
            xm = jnp.where(no_prev, 0.0, xm)
            xp = jnp.where(no_next, 0.0, xp)
        d = d_ref[:, cols]
        return xm * d[0:1] + x * d[1:2] + xp * d[2:3]

    for c0 in range(0, o_ref.shape[2], cw):
        cols = slice(c0, c0 + cw)
        gate = conv(wg_ref, dg_ref, cols)
        val = conv(wv_ref, dv_ref, cols)
        o_ref[0, :, cols] = (gate * jax.nn.sigmoid(gate) * val).astype(o_ref.dtype)


def ffn_up_conv_glu(h, up, dw, seq_len):
    g, t, d = h.shape
    f = up.shape[1] // 2
    assert dw.shape[0] == 3
    tm = _tile(t, 1024, FFN_HALO)
    tf = _tile(f, 512, LANES)
    nj = f // tf
    nh = tm // FFN_HALO
    n_halo_blocks = t // FFN_HALO
    w = jnp.zeros((SUBLANES, 2 * f), F32).at[:3].set(dw.astype(F32))
    return pl.pallas_call(
        functools.partial(_up_glu_kernel, seq_len=seq_len),
        grid=(g, t // tm, nj),
        in_specs=[pl.BlockSpec((1, tm, d), lambda b, i, j: (b, i, 0), pipeline_mode=pl.Buffered(1)),
                  pl.BlockSpec((1, FFN_HALO, d), lambda b, i, j: (b, jnp.maximum(i * nh - 1, 0), 0)),
                  pl.BlockSpec((1, FFN_HALO, d),
                               lambda b, i, j: (b, jnp.minimum((i + 1) * nh, n_halo_blocks - 1), 0)),
                  pl.BlockSpec((d, tf), lambda b, i, j: (0, j)),
                  pl.BlockSpec((d, tf), lambda b, i, j: (0, j + nj)),
                  pl.BlockSpec((SUBLANES, tf), lambda b, i, j: (0, j)),
                  pl.BlockSpec((SUBLANES, tf), lambda b, i, j: (0, j + nj))],
        out_specs=pl.BlockSpec((1, tm, tf), lambda b, i, j: (b, i, j)),
        out_shape=jax.ShapeDtypeStruct((g, t, f), BF16),
        scratch_shapes=[pltpu.VMEM((tm + 2 * FFN_HALO, d), BF16)],
        compiler_params=_params("parallel", "parallel", "arbitrary"),
        name="ffn_up_conv_glu",
    )(h, h, h, up, up, w, w)


def _mods(mods_l, rows, d):
    m = mods_l[rows[0]:rows[1]]
    return [m[:, k * d:(k + 1) * d].reshape(-1, 1, d) for k in range(6)]


def _ffn_block(x, mods, norm_g, up, dw, down, seq_len):
    sh2, sc2, g2 = mods
    h = modulate(x, norm_g, sc2, sh2)
    a = ffn_up_conv_glu(h, up, dw, seq_len)
    return matmul_residual(a, down, x, g2)


def kernel(x_prompt, x_sample, cache_k_a, cache_v_a, state_ret_fwd, state_ret_bwd, c, c_ctx, adaln_w, adaln_b, norm1_g, norm2_g, w_in_ab, w_out_ab, q_norm_g, k_norm_g, lambda_q1, lambda_k1, lambda_q2, lambda_k2, subln_g, ret_decay_exp_fwd, ret_decay_exp_bwd, conv_pw1, conv_dw, conv_ln_g, conv_ln_b, conv_pw2, ffn_up, ffn_dw, ffn_down):
    batch, seq, d = x_prompt.shape
    dec_batch, dec_seq, _ = x_sample.shape
    depth = adaln_w.shape[0]
    _, _, past_len, h_a, dk_a = cache_k_a.shape
    dh_a = dk_a // 2
    dv_a = cache_v_a.shape[-1]
    _, _, h_b, dk_b, dv_b = state_ret_fwd.shape
    assert dk_a == LANES and dv_a == LANES and dk_b == LANES and dv_b == LANES
    assert dec_batch + 1 <= SUBLANES
    w_qa, w_va, w_qb, w_vb = h_a * dk_a, h_a * dv_a, h_b * dk_b, h_b * dv_b
    col = {}
    off = 0
    for name, width in (("qa", w_qa), ("ka", w_qa), ("va", w_va), ("qb", w_qb), ("kb", w_qb),
                        ("vb", w_vb), ("gb", w_vb)):
        col[name] = off
        off += width

    conds = jnp.zeros((SUBLANES, d), F32).at[0].set(c_ctx).at[1:1 + dec_batch].set(c)
    mods_all = adaln(conds, adaln_w, adaln_b)

    layers_bf16 = lambda w: [w[j].astype(BF16) for j in range(w.shape[0])]
    w_in_bf, w_out_bf = layers_bf16(w_in_ab), layers_bf16(w_out_ab)
    pw1_bf, pw2_bf = layers_bf16(conv_pw1), layers_bf16(conv_pw2)
    up_bf, down_bf = layers_bf16(ffn_up), layers_bf16(ffn_down)

    rope_a = rope_tables(dec_seq, dh_a)
    rope_b = rope_tables(dec_seq, dk_b)

    def run(x, rows, seq_len, latent):
        g, t, _ = x.shape
        n_seq = g * t // seq_len
        new_ctx = []
        for l in range(depth):
            sh1, sc1, g1, sh2, sc2, g2 = _mods(mods_all[l], rows, d)
            h = modulate(x, norm1_g[l], sc1, sh1)
            i = l // 2
            if l % 2 == 0:
                proj = matmul(h, w_in_bf[i], F32)
                ra = rope_a if latent else None
                rb = rope_b if latent else None
                qa = head_prep(proj, col["qa"], w_qa, gain=q_norm_g[i], group=dh_a, rope=ra,
                               postscale=dh_a ** -0.5 * math.log2(math.e))
                ka = head_prep(proj, col["ka"], w_qa, gain=k_norm_g[i], group=dh_a, rope=ra,
                               want_f32=not latent)
                va = head_prep(proj, col["va"], w_va, want_f32=not latent)
                qb = head_prep(proj, col["qb"], w_qb, rope=rb)
                kb = head_prep(proj, col["kb"], w_qb, rope=rb, prescale=dk_b ** -0.5)
                vb = head_prep(proj, col["vb"], w_vb)
                seqs = lambda z: z.reshape(n_seq, seq_len, z.shape[-1])
                if latent:
                    k_all = jnp.concatenate(
                        [cache_k_a[:, i].reshape(dec_batch, past_len, w_qa).astype(BF16), ka], axis=1)
                    v_all = jnp.concatenate(
                        [cache_v_a[:, i].reshape(dec_batch, past_len, w_va).astype(BF16), va], axis=1)
                    s0f, s0b = state_ret_fwd[:, i], state_ret_bwd[:, i]
                else:
                    ka, ka_f32 = ka
                    va, va_f32 = va
                    k_all, v_all = seqs(ka), seqs(va)
                    s0f = jnp.zeros((n_seq, h_b, dk_b, dv_b), F32)
                    s0b = s0f
                lam_init = 0.8 - 0.6 * math.exp(-0.3 * l)
                lam_params = jnp.stack([lambda_q1[i], lambda_k1[i], lambda_q2[i], lambda_k2[i]]).astype(F32)
                oa = diff_attention(seqs(qa), k_all, v_all, lam_params, subln_g[i], lam_init, dh_a)
                dexp = jnp.stack([ret_decay_exp_fwd[i], ret_decay_exp_bwd[i]], axis=1).astype(F32)
                dexp = jnp.broadcast_to(dexp[:, :, None], (h_b, 2, LANES))
                ob, s_f, s_b = bi_retention(seqs(qb), seqs(kb), seqs(vb), seqs(proj), col["gb"], dexp,
                                            s0f, s0b)
                mix = jnp.concatenate([oa.reshape(g, t, w_va), ob.reshape(g, t, w_vb)], axis=-1)
                x = matmul_residual(mix, w_out_bf[i], x, g1)
                if not latent:
                    new_ctx.append((ka_f32.reshape(n_seq, seq_len, h_a, dk_a),
                                    va_f32.reshape(n_seq, seq_len, h_a, dv_a), s_f, s_b))
            else:
                u = matmul_glu(h, pw1_bf[i], F32)
                z = conv_ln_silu(u, conv_dw[i], conv_ln_g[i], conv_ln_b[i], seq_len)
                x = matmul_residual(z, pw2_bf[i], x, g1)
            x = _ffn_block(x, (sh2, sc2, g2), norm2_g[l], up_bf[l], ffn_dw[l], down_bf[l], seq_len)
        return x, new_ctx

    y_ctx, new_ctx = run(x_prompt.reshape(1, batch * seq, d), (0, 1), seq, False)
    y_lat, _ = run(x_sample, (1, 1 + dec_batch), dec_seq, True)
    stack = lambda k: jnp.stack([n[k] for n in new_ctx], axis=1)
    return (y_ctx.reshape(batch, seq, d), y_lat, stack(0), stack(1), stack(2), stack(3))
```

```python
import functools
import math

import jax
import jax.numpy as jnp
from jax import lax
from jax.experimental import pallas as pl
from jax.experimental.pallas import tpu as pltpu

F32 = jnp.float32
BF16 = jnp.bfloat16
EPS = 1e-6
GRID_W = 64
ROPE_BASE = 10000.0
LANES = 128
SUBLANES = 8
MXU_COLS = 256
CONV_HALO = 16
VMEM_LIMIT_V7X = 56 * 1024 * 1024
RET_CHUNK = 256


def _tile(dim, pref, align):
    if dim <= pref:
        return dim
    t = (pref // align) * align
    while t >= align:
        if dim % t == 0:
            return t
        t -= align
    return dim


def _params(*sem):
    return pltpu.CompilerParams(dimension_semantics=sem, vmem_limit_bytes=VMEM_LIMIT_V7X)


def _adaln_kernel(c_ref, w_ref, b_ref, o_ref):
    c = c_ref[...]
    s = (c * jax.nn.sigmoid(c)).astype(BF16)
    w = w_ref[0].astype(BF16)
    o_ref[0] = jnp.dot(s, w, preferred_element_type=F32) + b_ref[0]


def adaln(conds, w, b):
    depth, d, n = w.shape
    tn = _tile(n, 512, LANES)
    return pl.pallas_call(
        _adaln_kernel,
        grid=(depth, n // tn),
        in_specs=[pl.BlockSpec((SUBLANES, d), lambda l, j: (0, 0)),
                  pl.BlockSpec((1, d, tn), lambda l, j: (l, 0, j)),
                  pl.BlockSpec((1, 1, tn), lambda l, j: (l, 0, j))],
        out_specs=pl.BlockSpec((1, SUBLANES, tn), lambda l, j: (l, 0, j)),
        out_shape=jax.ShapeDtypeStruct((depth, SUBLANES, n), F32),
        compiler_params=_params("parallel", "parallel"),
        name="adaln",
    )(conds, w, b.reshape(depth, 1, n))


def _modulate_kernel(x_ref, g_ref, sc_ref, sh_ref, o_ref):
    d = x_ref.shape[2]
    cw = _tile(d, 512, LANES)
    chunks = [slice(c0, c0 + cw) for c0 in range(0, d, cw)]
    ss = sum(jnp.sum(jnp.square(x_ref[0, :, cs]), axis=-1, keepdims=True) for cs in chunks)
    inv = lax.rsqrt(ss * (1.0 / d) + EPS)
    for cs in chunks:
        y = x_ref[0, :, cs] * inv * g_ref[:, cs]
        o_ref[0, :, cs] = (y * (1.0 + sc_ref[0, :, cs]) + sh_ref[0, :, cs]).astype(o_ref.dtype)


def modulate(x, gain, scale, shift):
    g, t, d = x.shape
    tm = _tile(t, 256, SUBLANES)
    return pl.pallas_call(
        _modulate_kernel,
        grid=(g, t // tm),
        in_specs=[pl.BlockSpec((1, tm, d), lambda b, i: (b, i, 0)),
                  pl.BlockSpec((1, d), lambda b, i: (0, 0)),
                  pl.BlockSpec((1, 1, d), lambda b, i: (b, 0, 0)),
                  pl.BlockSpec((1, 1, d), lambda b, i: (b, 0, 0))],
        out_specs=pl.BlockSpec((1, tm, d), lambda b, i: (b, i, 0)),
        out_shape=jax.ShapeDtypeStruct((g, t, d), BF16),
        compiler_params=_params("parallel", "parallel"),
        name="modulate",
    )(x, gain.reshape(1, d), scale, shift)


def _mm_kernel(a_ref, w_ref, o_ref):
    o_ref[0] = jnp.dot(a_ref[0], w_ref[...], preferred_element_type=F32).astype(o_ref.dtype)


def _mm_res_kernel(*refs):
    *a_refs, w_ref, r_ref, g_ref, o_ref = refs
    acc, k0 = None, 0
    for a_ref in a_refs:
        k1 = k0 + a_ref.shape[2]
        part = jnp.dot(a_ref[0], w_ref[k0:k1, :], preferred_element_type=F32)
        acc = part if acc is None else acc + part
        k0 = k1
    o_ref[0] = r_ref[0] + g_ref[0] * acc


def _mm_glu_kernel(a_ref, wa_ref, wg_ref, o_ref):
    a = jnp.dot(a_ref[0], wa_ref[...], preferred_element_type=F32)
    g = jnp.dot(a_ref[0], wg_ref[...], preferred_element_type=F32)
    o_ref[0] = (a * jax.nn.sigmoid(g)).astype(o_ref.dtype)


def _mm_tiles(t, k, n):
    tm = _tile(t, 1024, SUBLANES)
    tn = _tile(n, 512 if k <= 8192 else 256, LANES)
    return tm, tn


def _lhs_spec(tm, k_piece, k_total):
    mode = pl.Buffered(1) if k_total > 8192 else None
    return pl.BlockSpec((1, tm, k_piece), lambda b, i, j: (b, i, 0), pipeline_mode=mode)


def _w_spec(k, tn, layer, col_block_offset=0):
    return pl.BlockSpec((None, k, tn), lambda b, i, j: (layer, 0, j + col_block_offset))


def matmul(a, w, layer, out_dtype):
    g, t, k = a.shape
    n = w.shape[2]
    tm, tn = _mm_tiles(t, k, n)
    return pl.pallas_call(
        _mm_kernel,
        grid=(g, t // tm, n // tn),
        in_specs=[pl.BlockSpec((1, tm, k), lambda b, i, j: (b, i, 0)),
                  _w_spec(k, tn, layer)],
        out_specs=pl.BlockSpec((1, tm, tn), lambda b, i, j: (b, i, j)),
        out_shape=jax.ShapeDtypeStruct((g, t, n), out_dtype),
        compiler_params=_params("parallel", "parallel", "parallel"),
        name="matmul",
    )(a, w)


def matmul_residual(a_pieces, w, layer, res, gate):
    g, t, _ = a_pieces[0].shape
    k = sum(a.shape[2] for a in a_pieces)
    n = w.shape[2]
    assert k == w.shape[1]
    tm, tn = _mm_tiles(t, k, n)
    return pl.pallas_call(
        _mm_res_kernel,
        grid=(g, t // tm, n // tn),
        in_specs=[_lhs_spec(tm, a.shape[2], k) for a in a_pieces]
                 + [_w_spec(k, tn, layer),
                    pl.BlockSpec((1, tm, tn), lambda b, i, j: (b, i, j)),
                    pl.BlockSpec((1, 1, tn), lambda b, i, j: (b, 0, j))],
        out_specs=pl.BlockSpec((1, tm, tn), lambda b, i, j: (b, i, j)),
        out_shape=jax.ShapeDtypeStruct((g, t, n), F32),
        compiler_params=_params("parallel", "parallel", "parallel"),
        name="matmul_residual",
    )(*a_pieces, w, res, gate)


def matmul_glu(a, w, layer, out_dtype):
    g, t, k = a.shape
    n = w.shape[2] // 2
    tm, tn = _mm_tiles(t, k, n)
    nj = n // tn
    return pl.pallas_call(
        _mm_glu_kernel,
        grid=(g, t // tm, nj),
        in_specs=[pl.BlockSpec((1, tm, k), lambda b, i, j: (b, i, 0)),
                  _w_spec(k, tn, layer),
                  _w_spec(k, tn, layer, nj)],
        out_specs=pl.BlockSpec((1, tm, tn), lambda b, i, j: (b, i, j)),
        out_shape=jax.ShapeDtypeStruct((g, t, n), out_dtype),
        compiler_params=_params("parallel", "parallel", "parallel"),
        name="matmul_glu",
    )(a, w, w)


def _prep_kernel(*refs, norm, rope_q, prescale, postscale, want_f32, group):
    refs = list(refs)
    x_ref = refs.pop(0)
    if norm:
        g_ref = refs.pop(0)
        bd_ref = refs.pop(0)
    if rope_q:
        cos_ref = refs.pop(0)
        sa_ref = refs.pop(0)
        sb_ref = refs.pop(0)
    o_ref = refs.pop(0)
    f_ref = refs.pop(0) if want_f32 else None
    for hh in range(x_ref.shape[2] // LANES):
        cols = slice(hh * LANES, (hh + 1) * LANES)
        y = x_ref[0, :, cols].astype(F32)
        if norm:
            sq = y * y
            hi = sq.astype(BF16)
            lo = (sq - hi.astype(F32)).astype(BF16)
            ss = (jnp.dot(hi, bd_ref[...], preferred_element_type=F32)
                  + jnp.dot(lo, bd_ref[...], preferred_element_type=F32))
            y = y * lax.rsqrt(ss * (1.0 / group) + EPS) * g_ref[...]
        if prescale != 1.0:
            y = y * prescale
        if rope_q:
            y = (y * cos_ref[...]
                 + pltpu.roll(y, LANES - rope_q, axis=1) * sa_ref[...]
                 + pltpu.roll(y, rope_q, axis=1) * sb_ref[...])
        if want_f32:
            f_ref[0, :, cols] = y
        if postscale != 1.0:
            y = y * postscale
        o_ref[0, :, cols] = y.astype(o_ref.dtype)


def head_prep(proj, col_start, width, *, gain=None, group=None, rope=None, prescale=1.0,
              postscale=1.0, want_f32=False):
    g, t, _ = proj.shape
    cw = _tile(width, 1024, LANES)
    tm = _tile(t, 512, SUBLANES)
    c0 = col_start // cw
    assert col_start % cw == 0
    in_specs = [pl.BlockSpec((1, tm, cw), lambda b, i, j: (b, i, c0 + j))]
    args = [proj]
    if gain is not None:
        lane = jnp.arange(LANES)
        bd = (lane[:, None] // group == lane[None, :] // group).astype(BF16)
        in_specs += [pl.BlockSpec((1, LANES), lambda b, i, j: (0, 0)),
                     pl.BlockSpec((LANES, LANES), lambda b, i, j: (0, 0))]
        args += [jnp.tile(gain.astype(F32), LANES // group).reshape(1, LANES), bd]
    rope_q = 0
    if rope is not None:
        rope_q, cos, sin_a, sin_b = rope
        in_specs += [pl.BlockSpec((tm, LANES), lambda b, i, j: (i, 0))] * 3
        args += [cos, sin_a, sin_b]
    out_specs = [pl.BlockSpec((1, tm, cw), lambda b, i, j: (b, i, j))]
    out_shape = [jax.ShapeDtypeStruct((g, t, width), BF16)]
    if want_f32:
        out_specs.append(pl.BlockSpec((1, tm, cw), lambda b, i, j: (b, i, j)))
        out_shape.append(jax.ShapeDtypeStruct((g, t, width), F32))
    outs = pl.pallas_call(
        functools.partial(_prep_kernel, norm=gain is not None, rope_q=rope_q, prescale=prescale,
                          postscale=postscale, want_f32=want_f32, group=group),
        grid=(g, t // tm, width // cw),
        in_specs=in_specs,
        out_specs=out_specs,
        out_shape=out_shape,
        compiler_params=_params("parallel", "parallel", "parallel"),
        name="head_prep",
    )(*args)
    return outs if want_f32 else outs[0]


def rope_tables(n_tokens, dim):
    rows = n_tokens // GRID_W
    row = jnp.repeat(jnp.arange(rows, dtype=F32), GRID_W)
    col = jnp.tile(jnp.arange(GRID_W, dtype=F32), rows)
    n_freq = dim // 4
    inv = ROPE_BASE ** (-jnp.arange(n_freq, dtype=F32) / n_freq)
    ar = row[:, None] * inv
    ac = col[:, None] * inv
    ang = jnp.concatenate([ar, ar, ac, ac], axis=-1)
    cos, sin = jnp.cos(ang), jnp.sin(ang)
    even = ((jnp.arange(dim) // n_freq) % 2 == 0)[None, :]
    sin_a = jnp.where(even, -sin, 0.0)
    sin_b = jnp.where(even, 0.0, sin)
    rep = LANES // dim
    return n_freq, jnp.tile(cos, (1, rep)), jnp.tile(sin_a, (1, rep)), jnp.tile(sin_b, (1, rep))


ATTN_ROWS = 128
ATTN_KEYS = 256


def _attn_kernel(q_ref, k_ref, v_ref, lp_ref, g_ref, o_ref, p_ref, s_ref, *, lam_init, dh):
    lk = k_ref.shape[1]
    tq = min(q_ref.shape[1], ATTN_ROWS)
    tk = _tile(lk, ATTN_KEYS, LANES)
    lp = lp_ref[...]
    lam = (jnp.exp(jnp.sum(lp[0:1] * lp[1:2], axis=1, keepdims=True))
           - jnp.exp(jnp.sum(lp[2:3] * lp[3:4], axis=1, keepdims=True)) + lam_init)
    dim1 = (((1,), (1,)), ((), ()))
    lane_tiles = [slice(j, j + LANES) for j in range(0, tk, LANES)]
    tile = 0
    for hh in range(q_ref.shape[2] // LANES):
        cols = slice(hh * LANES, (hh + 1) * LANES)
        for q0 in range(0, q_ref.shape[1], tq):
            pb_ref = p_ref.at[tile % 2]
            sb_ref = s_ref.at[tile % 2]
            tile += 1
            q = q_ref[0, q0:q0 + tq, cols]
            lane = lax.broadcasted_iota(jnp.int32, q.shape, 1)
            zero = jnp.zeros_like(q)
            qs = jnp.concatenate([jnp.where(lane < dh, q, zero), jnp.where(lane >= dh, q, zero)], axis=0)
            mx = jnp.full((2 * tq, LANES), -jnp.inf, F32)
            for k0 in range(0, lk, tk):
                s = lax.dot_general(qs, k_ref[0, k0:k0 + tk, cols], dim1, preferred_element_type=F32)
                sb_ref[:, k0:k0 + tk] = s
                for lt in lane_tiles:
                    mx = jnp.maximum(mx, s[:, lt])
            m = jnp.broadcast_to(jnp.max(mx, axis=1, keepdims=True), (2 * tq, LANES))
            ls = jnp.zeros((2 * tq, LANES), F32)
            for k0 in range(0, lk, LANES):
                p = jnp.exp2(sb_ref[:, k0:k0 + LANES] - m)
                ls = ls + p
                pb_ref[:, k0:k0 + LANES] = p.astype(BF16)
            l = jnp.sum(ls, axis=1, keepdims=True)
            on = jnp.dot(pb_ref[...], v_ref[0, :, cols], preferred_element_type=F32) / l
            o = on[0:tq] - lam * on[tq:2 * tq]
            ms = jnp.mean(o * o, axis=-1, keepdims=True)
            o = o * lax.rsqrt(ms + EPS) * g_ref[...] * (1.0 - lam_init)
            o_ref[0, q0:q0 + tq, cols] = o.astype(o_ref.dtype)


def diff_attention(q, k, v, lam_params, subln_g, lam_init, dh):
    b, lq, w = q.shape
    lk = k.shape[1]
    h = w // LANES
    tq = _tile(lq, 4 * ATTN_ROWS, SUBLANES)
    hb = _tile(h, 4, 1) if lk <= 2 * ATTN_KEYS else 1
    wb = hb * LANES
    return pl.pallas_call(
        functools.partial(_attn_kernel, lam_init=lam_init, dh=dh),
        grid=(b, h // hb, lq // tq),
        in_specs=[pl.BlockSpec((1, tq, wb), lambda bi, hi, qi: (bi, qi, hi)),
                  pl.BlockSpec((1, lk, wb), lambda bi, hi, qi: (bi, 0, hi)),
                  pl.BlockSpec((1, lk, wb), lambda bi, hi, qi: (bi, 0, hi)),
                  pl.BlockSpec((4, dh), lambda bi, hi, qi: (0, 0)),
                  pl.BlockSpec((1, LANES), lambda bi, hi, qi: (0, 0))],
        out_specs=pl.BlockSpec((1, tq, wb), lambda bi, hi, qi: (bi, qi, hi)),
        out_shape=jax.ShapeDtypeStruct((b, lq, w), BF16),
        scratch_shapes=[pltpu.VMEM((2, 2 * min(tq, ATTN_ROWS), lk), BF16),
                        pltpu.VMEM((2, 2 * min(tq, ATTN_ROWS), lk), F32)],
        compiler_params=_params("parallel", "parallel", "parallel"),
        name="diff_attention",
    )(q, k, v, lam_params, subln_g.astype(F32).reshape(1, LANES))


def _retention_kernel(q_ref, k_ref, v_ref, gate_ref, dx_ref, s0f_ref, s0b_ref,
                      o_ref, sf_ref, sb_ref, acc_ref, accb_ref, *, chunk):
    seq = q_ref.shape[1]
    n_chunks = seq // chunk
    dx = dx_ref[0]
    lg = jnp.log1p(-jnp.exp2(-dx))
    lgf, lgb = lg[0:1], lg[1:2]
    ii = lax.broadcasted_iota(jnp.int32, (chunk, chunk), 0)
    jj = lax.broadcasted_iota(jnp.int32, (chunk, chunk), 1)
    d = (ii - jj).astype(F32)
    dmat = (jnp.where(d >= 0, jnp.exp(jnp.maximum(d, 0.0) * lgf[:, 0:1]), 0.0)
            + jnp.where(d <= 0, jnp.exp(jnp.maximum(-d, 0.0) * lgb[:, 0:1]), 0.0))
    pos = lax.broadcasted_iota(jnp.int32, (chunk, LANES), 0).astype(F32)
    qdec_f = jnp.exp((pos + 1.0) * lgf)
    kdec_f = jnp.exp((chunk - 1.0 - pos) * lgf)
    cdec_f = jnp.exp(chunk * lgf)
    qdec_b = jnp.exp((chunk - pos) * lgb)
    kdec_b = jnp.exp(pos * lgb)
    cdec_b = jnp.exp(chunk * lgb)
    dim0 = (((0,), (0,)), ((), ()))
    dim1 = (((1,), (1,)), ((), ()))

    def step(n, carry):
        s_f, s_b = carry
        rows = pl.ds(pl.multiple_of(n * chunk, chunk), chunk)
        q, k, v = q_ref[0, rows, :], k_ref[0, rows, :], v_ref[0, rows, :]
        att = lax.dot_general(q, k, dim1, preferred_element_type=F32) * dmat
        acc_ref[rows, :] = (jnp.dot(att.astype(BF16), v, preferred_element_type=F32)
                            + jnp.dot((q * qdec_f).astype(BF16), s_f.astype(BF16),
                                      preferred_element_type=F32))
        s_f = s_f * cdec_f + lax.dot_general((k * kdec_f).astype(BF16), v, dim0,
                                             preferred_element_type=F32)
        rows = pl.ds(pl.multiple_of((n_chunks - 1 - n) * chunk, chunk), chunk)
        q, k, v = q_ref[0, rows, :], k_ref[0, rows, :], v_ref[0, rows, :]
        accb_ref[rows, :] = jnp.dot((q * qdec_b).astype(BF16), s_b.astype(BF16),
                                    preferred_element_type=F32)
        s_b = s_b * cdec_b + lax.dot_general((k * kdec_b).astype(BF16), v, dim0,
                                             preferred_element_type=F32)
        return s_f, s_b

    sf_ref[0, 0], sb_ref[0, 0] = lax.fori_loop(0, n_chunks, step, (s0f_ref[0, 0], s0b_ref[0, 0]))

    o = acc_ref[...] + accb_ref[...]
    ms = jnp.mean(o * o, axis=-1, keepdims=True)
    gate = gate_ref[0].astype(F32)
    o_ref[0] = (o * lax.rsqrt(ms + EPS) * (gate * jax.nn.sigmoid(gate))).astype(o_ref.dtype)


def bi_retention(q, k, v, proj, gate_col, dexp, s0f, s0b):
    b, seq, w = q.shape
    h = w // LANES
    chunk = _tile(seq, RET_CHUNK, LANES)
    gc = gate_col // LANES
    head = lambda bi, hi: (bi, 0, hi)
    state = lambda bi, hi: (bi, hi, 0, 0)
    return pl.pallas_call(
        functools.partial(_retention_kernel, chunk=chunk),
        grid=(b, h),
        in_specs=[pl.BlockSpec((1, seq, LANES), head),
                  pl.BlockSpec((1, seq, LANES), head),
                  pl.BlockSpec((1, seq, LANES), head),
                  pl.BlockSpec((1, seq, LANES), lambda bi, hi: (bi, 0, gc + hi)),
                  pl.BlockSpec((1, 2, LANES), lambda bi, hi: (hi, 0, 0)),
                  pl.BlockSpec((1, 1, LANES, LANES), state),
                  pl.BlockSpec((1, 1, LANES, LANES), state)],
        out_specs=[pl.BlockSpec((1, seq, LANES), head),
                   pl.BlockSpec((1, 1, LANES, LANES), state),
                   pl.BlockSpec((1, 1, LANES, LANES), state)],
        out_shape=[jax.ShapeDtypeStruct((b, seq, w), BF16),
                   jax.ShapeDtypeStruct((b, h, LANES, LANES), F32),
                   jax.ShapeDtypeStruct((b, h, LANES, LANES), F32)],
        scratch_shapes=[pltpu.VMEM((seq, LANES), F32), pltpu.VMEM((seq, LANES), F32)],
        compiler_params=_params("parallel", "parallel"),
        name="bi_retention",
    )(q, k, v, proj, dexp, s0f, s0b)


def _conv_ln_kernel(u_ref, prev_ref, next_ref, w_ref, g_ref, b_ref, o_ref, ext_ref, y_ref, shift_ref, *,
                    seq_len, taps, row_block):
    i = pl.program_id(1)
    c = pl.program_id(2)
    tm, cw = u_ref.shape[1], u_ref.shape[2]
    first = (i * tm) % seq_len == 0
    last = ((i + 1) * tm) % seq_len == 0
    zero = jnp.zeros((CONV_HALO, cw), F32)
    ext_ref[0:CONV_HALO, :] = jnp.where(first, zero, prev_ref[0])
    ext_ref[CONV_HALO:CONV_HALO + tm, :] = u_ref[0]
    ext_ref[CONV_HALO + tm:, :] = jnp.where(last, zero, next_ref[0])
    half = (taps - 1) // 2
    for r0 in range(0, tm, row_block):
        acc = None
        for b in range(SUBLANES):
            part = None
            for a in range(-((half + b) // SUBLANES), (half - b) // SUBLANES + 1):
                o = b + SUBLANES * a
                term = (ext_ref[pl.ds(CONV_HALO + r0 + SUBLANES * a, row_block + SUBLANES), :]
                        * w_ref[half + o:half + o + 1, :])
                part = term if part is None else part + term
            if b == 0:
                shifted = part[0:row_block]
            else:
                shift_ref[b % 2] = part
                shifted = shift_ref[b % 2, pl.ds(b, row_block), :]
            acc = shifted if acc is None else acc + shifted
        y_ref[c, pl.ds(r0, row_block), :] = acc

    @pl.when(c == pl.num_programs(2) - 1)
    def _():
        nc = y_ref.shape[0]
        d = nc * cw
        mu = sum(jnp.sum(y_ref[cc], axis=-1, keepdims=True) for cc in range(nc)) * (1.0 / d)
        var = sum(jnp.sum(jnp.square(y_ref[cc] - mu), axis=-1, keepdims=True)
                  for cc in range(nc)) * (1.0 / d)
        inv = lax.rsqrt(var + EPS)
        for cc in range(nc):
            cols = slice(cc * cw, (cc + 1) * cw)
            z = (y_ref[cc] - mu) * inv * g_ref[:, cols] + b_ref[:, cols]
            o_ref[0, :, cols] = (z * jax.nn.sigmoid(z)).astype(o_ref.dtype)


def conv_ln_silu(u, dw, ln_g, ln_b, seq_len):
    g, t, d = u.shape
    taps = dw.shape[0]
    half = (taps - 1) // 2
    assert taps % 2 == 1 and SUBLANES * (-(-half // SUBLANES)) <= CONV_HALO
    assert SUBLANES * (half // SUBLANES + 1) <= CONV_HALO
    tm = _tile(min(t, seq_len), 256, CONV_HALO)
    cw = _tile(d, 512, LANES)
    nh = tm // CONV_HALO
    n_halo_blocks = t // CONV_HALO
    w = jnp.zeros((2 * CONV_HALO, d), F32).at[:taps].set(dw.astype(F32))
    return pl.pallas_call(
        functools.partial(_conv_ln_kernel, seq_len=seq_len, taps=taps, row_block=min(tm, 32)),
        grid=(g, t // tm, d // cw),
        in_specs=[pl.BlockSpec((1, tm, cw), lambda b, i, c: (b, i, c)),
                  pl.BlockSpec((1, CONV_HALO, cw), lambda b, i, c: (b, jnp.maximum(i * nh - 1, 0), c)),
                  pl.BlockSpec((1, CONV_HALO, cw),
                               lambda b, i, c: (b, jnp.minimum((i + 1) * nh, n_halo_blocks - 1), c)),
                  pl.BlockSpec((2 * CONV_HALO, cw), lambda b, i, c: (0, c)),
                  pl.BlockSpec((1, d), lambda b, i, c: (0, 0)),
                  pl.BlockSpec((1, d), lambda b, i, c: (0, 0))],
        out_specs=pl.BlockSpec((1, tm, d), lambda b, i, c: (b, i, 0)),
        out_shape=jax.ShapeDtypeStruct((g, t, d), BF16),
        scratch_shapes=[pltpu.VMEM((tm + 2 * CONV_HALO, cw), F32),
                        pltpu.VMEM((d // cw, tm, cw), F32),
                        pltpu.VMEM((2, min(tm, 32) + SUBLANES, cw), F32)],
        compiler_params=_params("parallel", "parallel", "arbitrary"),
        name="conv_ln_silu",
    )(u, u, u, w, ln_g.astype(F32).reshape(1, d), ln_b.astype(F32).reshape(1, d))


FFN_HALO = 16


def _patch_rows(arr, patches):
    sub = lax.broadcasted_iota(jnp.int32, (SUBLANES, arr.shape[1]), 0)
    pieces, done = [], 0
    for r, val in patches:
        g0 = r // SUBLANES * SUBLANES
        assert g0 >= done
        if g0 > done:
            pieces.append(arr[done:g0])
        pieces.append(jnp.where(sub == r % SUBLANES, val, arr[g0:g0 + SUBLANES]))
        done = g0 + SUBLANES
    if done < arr.shape[0]:
        pieces.append(arr[done:])
    return jnp.concatenate(pieces, axis=0)


def _up_glu_kernel(h_ref, hp_ref, hn_ref, wg_ref, wv_ref, dg_ref, dv_ref, o_ref, hx_ref, *, seq_len):
    i = pl.program_id(1)
    tm = h_ref.shape[1]
    halo = FFN_HALO

    @pl.when(pl.program_id(2) == 0)
    def _():
        hx_ref[0:halo, :] = hp_ref[0]
        hx_ref[halo:halo + tm, :] = h_ref[0]
        hx_ref[halo + tm:, :] = hn_ref[0]

    cw = MXU_COLS if o_ref.shape[2] % MXU_COLS == 0 else LANES
    tile_in_one_seq = seq_len % tm == 0
    if tile_in_one_seq:
        first = lax.rem(i * tm, seq_len) == 0
        last = lax.rem((i + 1) * tm, seq_len) == 0

    def conv(w_ref, d_ref, cols):
        p = jnp.dot(hx_ref[...], w_ref[:, cols], preferred_element_type=F32)
        x = p[halo:halo + tm]
        if tile_in_one_seq:
            starts = [(0, jnp.where(first, 0.0, p[halo - 1:halo]))]
            ends = [(tm - 1, jnp.where(last, 0.0, p[halo + tm:halo + tm + 1]))]
        else:
            starts = [(r, 0.0) for r in range(0, tm, seq_len)]
            ends = [(r + seq_len - 1, 0.0) for r in range(0, tm, seq_len)]
        xm = _patch_rows(pltpu.roll(x, 1, axis=0), starts)
        xp = _patch_rows(pltpu.roll(x, tm - 1, axis=0), ends)
        d = d_ref[:, cols]
        return xm * d[0:1] + x * d[1:2] + xp * d[2:3]

    for c0 in range(0, o_ref.shape[2], cw):
        cols = slice(c0, c0 + cw)
        gate = conv(wg_ref, dg_ref, cols)
        val = conv(wv_ref, dv_ref, cols)
        o_ref[0, :, cols] = (gate * jax.nn.sigmoid(gate) * val).astype(o_ref.dtype)


def ffn_up_conv_glu(h, up, layer, dw, seq_len):
    g, t, d = h.shape
    f = up.shape[2] // 2
    assert dw.shape[0] == 3
    tm = _tile(t, 1024, FFN_HALO)
    assert seq_len % SUBLANES == 0 and (seq_len % tm == 0 or tm % seq_len == 0)
    tf = _tile(f, 512, LANES)
    nj = f // tf
    nh = tm // FFN_HALO
    n_halo_blocks = t // FFN_HALO
    w = jnp.zeros((SUBLANES, 2 * f), F32).at[:3].set(dw.astype(F32))
    return pl.pallas_call(
        functools.partial(_up_glu_kernel, seq_len=seq_len),
        grid=(g, t // tm, nj),
        in_specs=[pl.BlockSpec((1, tm, d), lambda b, i, j: (b, i, 0), pipeline_mode=pl.Buffered(1)),
                  pl.BlockSpec((1, FFN_HALO, d), lambda b, i, j: (b, jnp.maximum(i * nh - 1, 0), 0)),
                  pl.BlockSpec((1, FFN_HALO, d),
                               lambda b, i, j: (b, jnp.minimum((i + 1) * nh, n_halo_blocks - 1), 0)),
                  _w_spec(d, tf, layer),
                  _w_spec(d, tf, layer, nj),
                  pl.BlockSpec((SUBLANES, tf), lambda b, i, j: (0, j)),
                  pl.BlockSpec((SUBLANES, tf), lambda b, i, j: (0, j + nj))],
        out_specs=pl.BlockSpec((1, tm, tf), lambda b, i, j: (b, i, j)),
        out_shape=jax.ShapeDtypeStruct((g, t, f), BF16),
        scratch_shapes=[pltpu.VMEM((tm + 2 * FFN_HALO, d), BF16)],
        compiler_params=_params("parallel", "parallel", "arbitrary"),
        name="ffn_up_conv_glu",
    )(h, h, h, up, up, w, w)


def _mods(mods_l, rows, d):
    m = mods_l[rows[0]:rows[1]]
    return [m[:, k * d:(k + 1) * d].reshape(-1, 1, d) for k in range(6)]


def _ffn_block(x, mods, norm_g, up, dw, down, layer, seq_len):
    sh2, sc2, g2 = mods
    h = modulate(x, norm_g, sc2, sh2)
    a = ffn_up_conv_glu(h, up, layer, dw, seq_len)
    return matmul_residual([a], down, layer, x, g2)


def kernel(x_prompt, x_sample, cache_k_a, cache_v_a, state_ret_fwd, state_ret_bwd, c, c_ctx, adaln_w, adaln_b, norm1_g, norm2_g, w_in_ab, w_out_ab, q_norm_g, k_norm_g, lambda_q1, lambda_k1, lambda_q2, lambda_k2, subln_g, ret_decay_exp_fwd, ret_decay_exp_bwd, conv_pw1, conv_dw, conv_ln_g, conv_ln_b, conv_pw2, ffn_up, ffn_dw, ffn_down):
    batch, seq, d = x_prompt.shape
    dec_batch, dec_seq, _ = x_sample.shape
    depth = adaln_w.shape[0]
    _, _, past_len, h_a, dk_a = cache_k_a.shape
    dh_a = dk_a // 2
    dv_a = cache_v_a.shape[-1]
    _, _, h_b, dk_b, dv_b = state_ret_fwd.shape
    assert dk_a == LANES and dv_a == LANES and dk_b == LANES and dv_b == LANES
    assert dec_batch + 1 <= SUBLANES
    w_qa, w_va, w_qb, w_vb = h_a * dk_a, h_a * dv_a, h_b * dk_b, h_b * dv_b
    col = {}
    off = 0
    for name, width in (("qa", w_qa), ("ka", w_qa), ("va", w_va), ("qb", w_qb), ("kb", w_qb),
                        ("vb", w_vb), ("gb", w_vb)):
        col[name] = off
        off += width

    conds = jnp.zeros((SUBLANES, d), F32).at[0].set(c_ctx).at[1:1 + dec_batch].set(c)
    mods_all = adaln(conds, adaln_w, adaln_b)

    w_in_bf, w_out_bf = w_in_ab.astype(BF16), w_out_ab.astype(BF16)
    pw1_bf, pw2_bf = conv_pw1.astype(BF16), conv_pw2.astype(BF16)
    up_bf, down_bf = ffn_up.astype(BF16), ffn_down.astype(BF16)

    rope_a = rope_tables(dec_seq, dh_a)
    rope_b = rope_tables(dec_seq, dk_b)

    def run(x, rows, seq_len, latent):
        g, t, _ = x.shape
        n_seq = g * t // seq_len
        new_ctx = []
        for l in range(depth):
            sh1, sc1, g1, sh2, sc2, g2 = _mods(mods_all[l], rows, d)
            h = modulate(x, norm1_g[l], sc1, sh1)
            i = l // 2
            if l % 2 == 0:
                proj = matmul(h, w_in_bf, i, F32)
                ra = rope_a if latent else None
                rb = rope_b if latent else None
                qa = head_prep(proj, col["qa"], w_qa, gain=q_norm_g[i], group=dh_a, rope=ra,
                               postscale=dh_a ** -0.5 * math.log2(math.e))
                ka = head_prep(proj, col["ka"], w_qa, gain=k_norm_g[i], group=dh_a, rope=ra,
                               want_f32=not latent)
                va = head_prep(proj, col["va"], w_va, want_f32=not latent)
                qb = head_prep(proj, col["qb"], w_qb, rope=rb)
                kb = head_prep(proj, col["kb"], w_qb, rope=rb, prescale=dk_b ** -0.5)
                vb = head_prep(proj, col["vb"], w_vb)
                seqs = lambda z: z.reshape(n_seq, seq_len, z.shape[-1])
                if latent:
                    k_all = jnp.concatenate(
                        [cache_k_a[:, i].reshape(dec_batch, past_len, w_qa).astype(BF16), ka], axis=1)
                    v_all = jnp.concatenate(
                        [cache_v_a[:, i].reshape(dec_batch, past_len, w_va).astype(BF16), va], axis=1)
                    s0f, s0b = state_ret_fwd[:, i], state_ret_bwd[:, i]
                else:
                    ka, ka_f32 = ka
                    va, va_f32 = va
                    k_all, v_all = seqs(ka), seqs(va)
                    s0f = jnp.zeros((n_seq, h_b, dk_b, dv_b), F32)
                    s0b = s0f
                lam_init = 0.8 - 0.6 * math.exp(-0.3 * l)
                lam_params = jnp.stack([lambda_q1[i], lambda_k1[i], lambda_q2[i], lambda_k2[i]]).astype(F32)
                oa = diff_attention(seqs(qa), k_all, v_all, lam_params, subln_g[i], lam_init, dh_a)
                dexp = jnp.stack([ret_decay_exp_fwd[i], ret_decay_exp_bwd[i]], axis=1).astype(F32)
                dexp = jnp.broadcast_to(dexp[:, :, None], (h_b, 2, LANES))
                ob, s_f, s_b = bi_retention(seqs(qb), seqs(kb), seqs(vb), seqs(proj), col["gb"], dexp,
                                            s0f, s0b)
                x = matmul_residual([oa.reshape(g, t, w_va), ob.reshape(g, t, w_vb)], w_out_bf, i, x, g1)
                if not latent:
                    new_ctx.append((ka_f32.reshape(n_seq, seq_len, h_a, dk_a),
                                    va_f32.reshape(n_seq, seq_len, h_a, dv_a), s_f, s_b))
            else:
                u = matmul_glu(h, pw1_bf, i, F32)
                z = conv_ln_silu(u, conv_dw[i], conv_ln_g[i], conv_ln_b[i], seq_len)
                x = matmul_residual([z], pw2_bf, i, x, g1)
            x = _ffn_block(x, (sh2, sc2, g2), norm2_g[l], up_bf, ffn_dw[l], down_bf, l, seq_len)
        return x, new_ctx

    y_ctx, new_ctx = run(x_prompt.reshape(1, batch * seq, d), (0, 1), seq, False)
    y_lat, _ = run(x_sample, (1, 1 + dec_batch), dec_seq, True)
    stack = lambda k: jnp.stack([n[k] for n in new_ctx], axis=1)
    return (y_ctx.reshape(batch, seq, d), y_lat, stack(0), stack(1), stack(2), stack(3))
```

```python
import functools
import math

import jax
import jax.numpy as jnp
from jax import lax
from jax.experimental import pallas as pl
from jax.experimental.pallas import tpu as pltpu

F32 = jnp.float32
BF16 = jnp.bfloat16
EPS = 1e-6
GRID_W = 64
ROPE_BASE = 10000.0
LANES = 128
SUBLANES = 8
MXU_COLS = 256
CONV_HALO = 16
VMEM_LIMIT_V7X = 56 * 1024 * 1024
RET_CHUNK = 256


def _tile(dim, pref, align):
    if dim <= pref:
        return dim
    t = (pref // align) * align
    while t >= align:
        if dim % t == 0:
            return t
        t -= align
    return dim


def _params(*sem):
    return pltpu.CompilerParams(dimension_semantics=sem, vmem_limit_bytes=VMEM_LIMIT_V7X)


def _adaln_kernel(c_ref, w_ref, b_ref, o_ref):
    c = c_ref[...]
    s = (c * jax.nn.sigmoid(c)).astype(BF16)
    w = w_ref[0].astype(BF16)
    o_ref[0] = jnp.dot(s, w, preferred_element_type=F32) + b_ref[0]


def adaln(conds, w, b):
    depth, d, n = w.shape
    tn = _tile(n, 512, LANES)
    return pl.pallas_call(
        _adaln_kernel,
        grid=(depth, n // tn),
        in_specs=[pl.BlockSpec((SUBLANES, d), lambda l, j: (0, 0)),
                  pl.BlockSpec((1, d, tn), lambda l, j: (l, 0, j)),
                  pl.BlockSpec((1, 1, tn), lambda l, j: (l, 0, j))],
        out_specs=pl.BlockSpec((1, SUBLANES, tn), lambda l, j: (l, 0, j)),
        out_shape=jax.ShapeDtypeStruct((depth, SUBLANES, n), F32),
        compiler_params=_params("parallel", "parallel"),
        name="adaln",
    )(conds, w, b.reshape(depth, 1, n))


def _modulate_kernel(x_ref, g_ref, sc_ref, sh_ref, o_ref):
    d = x_ref.shape[2]
    cw = _tile(d, 512, LANES)
    chunks = [slice(c0, c0 + cw) for c0 in range(0, d, cw)]
    ss = sum(jnp.sum(jnp.square(x_ref[0, :, cs]), axis=-1, keepdims=True) for cs in chunks)
    inv = lax.rsqrt(ss * (1.0 / d) + EPS)
    for cs in chunks:
        y = x_ref[0, :, cs] * inv * g_ref[:, cs]
        o_ref[0, :, cs] = (y * (1.0 + sc_ref[0, :, cs]) + sh_ref[0, :, cs]).astype(o_ref.dtype)


def modulate(x, gain, scale, shift):
    g, t, d = x.shape
    tm = _tile(t, 256, SUBLANES)
    return pl.pallas_call(
        _modulate_kernel,
        grid=(g, t // tm),
        in_specs=[pl.BlockSpec((1, tm, d), lambda b, i: (b, i, 0)),
                  pl.BlockSpec((1, d), lambda b, i: (0, 0)),
                  pl.BlockSpec((1, 1, d), lambda b, i: (b, 0, 0)),
                  pl.BlockSpec((1, 1, d), lambda b, i: (b, 0, 0))],
        out_specs=pl.BlockSpec((1, tm, d), lambda b, i: (b, i, 0)),
        out_shape=jax.ShapeDtypeStruct((g, t, d), BF16),
        compiler_params=_params("parallel", "parallel"),
        name="modulate",
    )(x, gain.reshape(1, d), scale, shift)


def _mm_kernel(a_ref, w_ref, o_ref):
    o_ref[0] = jnp.dot(a_ref[0], w_ref[...], preferred_element_type=F32).astype(o_ref.dtype)


def _mm_res_kernel(*refs):
    *a_refs, w_ref, r_ref, g_ref, o_ref = refs
    acc, k0 = None, 0
    for a_ref in a_refs:
        k1 = k0 + a_ref.shape[2]
        part = jnp.dot(a_ref[0], w_ref[k0:k1, :], preferred_element_type=F32)
        acc = part if acc is None else acc + part
        k0 = k1
    o_ref[0] = r_ref[0] + g_ref[0] * acc


def _mm_glu_kernel(a_ref, wa_ref, wg_ref, o_ref):
    a = jnp.dot(a_ref[0], wa_ref[...], preferred_element_type=F32)
    g = jnp.dot(a_ref[0], wg_ref[...], preferred_element_type=F32)
    o_ref[0] = (a * jax.nn.sigmoid(g)).astype(o_ref.dtype)


def _mm_tiles(t, k, n):
    tm = _tile(t, 1024, SUBLANES)
    tn = _tile(n, 512 if k <= 8192 else 256, LANES)
    return tm, tn


def _lhs_spec(tm, k_piece, k_total):
    mode = pl.Buffered(1) if k_total > 8192 else None
    return pl.BlockSpec((1, tm, k_piece), lambda b, i, j: (b, i, 0), pipeline_mode=mode)


def _w_spec(k, tn, layer, col_block_offset=0):
    return pl.BlockSpec((None, k, tn), lambda b, i, j: (layer, 0, j + col_block_offset))


def matmul(a, w, layer, out_dtype):
    g, t, k = a.shape
    n = w.shape[2]
    tm, tn = _mm_tiles(t, k, n)
    return pl.pallas_call(
        _mm_kernel,
        grid=(g, t // tm, n // tn),
        in_specs=[pl.BlockSpec((1, tm, k), lambda b, i, j: (b, i, 0)),
                  _w_spec(k, tn, layer)],
        out_specs=pl.BlockSpec((1, tm, tn), lambda b, i, j: (b, i, j)),
        out_shape=jax.ShapeDtypeStruct((g, t, n), out_dtype),
        compiler_params=_params("parallel", "parallel", "parallel"),
        name="matmul",
    )(a, w)


def matmul_residual(a_pieces, w, layer, res, gate):
    g, t, _ = a_pieces[0].shape
    k = sum(a.shape[2] for a in a_pieces)
    n = w.shape[2]
    assert k == w.shape[1]
    tm, tn = _mm_tiles(t, k, n)
    return pl.pallas_call(
        _mm_res_kernel,
        grid=(g, t // tm, n // tn),
        in_specs=[_lhs_spec(tm, a.shape[2], k) for a in a_pieces]
                 + [_w_spec(k, tn, layer),
                    pl.BlockSpec((1, tm, tn), lambda b, i, j: (b, i, j)),
                    pl.BlockSpec((1, 1, tn), lambda b, i, j: (b, 0, j))],
        out_specs=pl.BlockSpec((1, tm, tn), lambda b, i, j: (b, i, j)),
        out_shape=jax.ShapeDtypeStruct((g, t, n), F32),
        compiler_params=_params("parallel", "parallel", "parallel"),
        name="matmul_residual",
    )(*a_pieces, w, res, gate)


def matmul_glu(a, w, layer, out_dtype):
    g, t, k = a.shape
    n = w.shape[2] // 2
    tm, tn = _mm_tiles(t, k, n)
    nj = n // tn
    return pl.pallas_call(
        _mm_glu_kernel,
        grid=(g, t // tm, nj),
        in_specs=[pl.BlockSpec((1, tm, k), lambda b, i, j: (b, i, 0)),
                  _w_spec(k, tn, layer),
                  _w_spec(k, tn, layer, nj)],
        out_specs=pl.BlockSpec((1, tm, tn), lambda b, i, j: (b, i, j)),
        out_shape=jax.ShapeDtypeStruct((g, t, n), out_dtype),
        compiler_params=_params("parallel", "parallel", "parallel"),
        name="matmul_glu",
    )(a, w, w)


def _prep_kernel(*refs, norm, rope_q, prescale, postscale, want_f32, group):
    refs = list(refs)
    x_ref = refs.pop(0)
    if norm:
        g_ref = refs.pop(0)
        bd_ref = refs.pop(0)
    if rope_q:
        cos_ref = refs.pop(0)
        sa_ref = refs.pop(0)
        sb_ref = refs.pop(0)
    o_ref = refs.pop(0)
    f_ref = refs.pop(0) if want_f32 else None
    for hh in range(x_ref.shape[2] // LANES):
        cols = slice(hh * LANES, (hh + 1) * LANES)
        y = x_ref[0, :, cols].astype(F32)
        if norm:
            sq = y * y
            hi = sq.astype(BF16)
            lo = (sq - hi.astype(F32)).astype(BF16)
            ss = (jnp.dot(hi, bd_ref[...], preferred_element_type=F32)
                  + jnp.dot(lo, bd_ref[...], preferred_element_type=F32))
            y = y * lax.rsqrt(ss * (1.0 / group) + EPS) * g_ref[...]
        if prescale != 1.0:
            y = y * prescale
        if rope_q:
            y = (y * cos_ref[...]
                 + pltpu.roll(y, LANES - rope_q, axis=1) * sa_ref[...]
                 + pltpu.roll(y, rope_q, axis=1) * sb_ref[...])
        if want_f32:
            f_ref[0, :, cols] = y
        if postscale != 1.0:
            y = y * postscale
        o_ref[0, :, cols] = y.astype(o_ref.dtype)


def head_prep(proj, col_start, width, *, gain=None, group=None, rope=None, prescale=1.0,
              postscale=1.0, want_f32=False):
    g, t, _ = proj.shape
    cw = _tile(width, 1024, LANES)
    tm = _tile(t, 512, SUBLANES)
    c0 = col_start // cw
    assert col_start % cw == 0
    in_specs = [pl.BlockSpec((1, tm, cw), lambda b, i, j: (b, i, c0 + j))]
    args = [proj]
    if gain is not None:
        lane = jnp.arange(LANES)
        bd = (lane[:, None] // group == lane[None, :] // group).astype(BF16)
        in_specs += [pl.BlockSpec((1, LANES), lambda b, i, j: (0, 0)),
                     pl.BlockSpec((LANES, LANES), lambda b, i, j: (0, 0))]
        args += [jnp.tile(gain.astype(F32), LANES // group).reshape(1, LANES), bd]
    rope_q = 0
    if rope is not None:
        rope_q, cos, sin_a, sin_b = rope
        in_specs += [pl.BlockSpec((tm, LANES), lambda b, i, j: (i, 0))] * 3
        args += [cos, sin_a, sin_b]
    out_specs = [pl.BlockSpec((1, tm, cw), lambda b, i, j: (b, i, j))]
    out_shape = [jax.ShapeDtypeStruct((g, t, width), BF16)]
    if want_f32:
        out_specs.append(pl.BlockSpec((1, tm, cw), lambda b, i, j: (b, i, j)))
        out_shape.append(jax.ShapeDtypeStruct((g, t, width), F32))
    outs = pl.pallas_call(
        functools.partial(_prep_kernel, norm=gain is not None, rope_q=rope_q, prescale=prescale,
                          postscale=postscale, want_f32=want_f32, group=group),
        grid=(g, t // tm, width // cw),
        in_specs=in_specs,
        out_specs=out_specs,
        out_shape=out_shape,
        compiler_params=_params("parallel", "parallel", "parallel"),
        name="head_prep",
    )(*args)
    return outs if want_f32 else outs[0]


def rope_tables(n_tokens, dim):
    rows = n_tokens // GRID_W
    row = jnp.repeat(jnp.arange(rows, dtype=F32), GRID_W)
    col = jnp.tile(jnp.arange(GRID_W, dtype=F32), rows)
    n_freq = dim // 4
    inv = ROPE_BASE ** (-jnp.arange(n_freq, dtype=F32) / n_freq)
    ar = row[:, None] * inv
    ac = col[:, None] * inv
    ang = jnp.concatenate([ar, ar, ac, ac], axis=-1)
    cos, sin = jnp.cos(ang), jnp.sin(ang)
    even = ((jnp.arange(dim) // n_freq) % 2 == 0)[None, :]
    sin_a = jnp.where(even, -sin, 0.0)
    sin_b = jnp.where(even, 0.0, sin)
    rep = LANES // dim
    return n_freq, jnp.tile(cos, (1, rep)), jnp.tile(sin_a, (1, rep)), jnp.tile(sin_b, (1, rep))


ATTN_ROWS = 128
ATTN_KEYS = 256


def _attn_kernel(q_ref, k_ref, v_ref, lp_ref, g_ref, o_ref, p_ref, s_ref, *, lam_init, dh):
    lk = k_ref.shape[1]
    tq = min(q_ref.shape[1], ATTN_ROWS)
    tk = _tile(lk, ATTN_KEYS, LANES)
    lp = lp_ref[...]
    lam = (jnp.exp(jnp.sum(lp[0:1] * lp[1:2], axis=1, keepdims=True))
           - jnp.exp(jnp.sum(lp[2:3] * lp[3:4], axis=1, keepdims=True)) + lam_init)
    dim1 = (((1,), (1,)), ((), ()))
    lane_tiles = [slice(j, j + LANES) for j in range(0, tk, LANES)]
    tile = 0
    for hh in range(q_ref.shape[2] // LANES):
        cols = slice(hh * LANES, (hh + 1) * LANES)
        for q0 in range(0, q_ref.shape[1], tq):
            pb_ref = p_ref.at[tile % 2]
            sb_ref = s_ref.at[tile % 2]
            tile += 1
            q = q_ref[0, q0:q0 + tq, cols]
            lane = lax.broadcasted_iota(jnp.int32, q.shape, 1)
            zero = jnp.zeros_like(q)
            qs = jnp.concatenate([jnp.where(lane < dh, q, zero), jnp.where(lane >= dh, q, zero)], axis=0)
            mx = jnp.full((2 * tq, LANES), -jnp.inf, F32)
            for k0 in range(0, lk, tk):
                s = lax.dot_general(qs, k_ref[0, k0:k0 + tk, cols], dim1, preferred_element_type=F32)
                sb_ref[:, k0:k0 + tk] = s
                for lt in lane_tiles:
                    mx = jnp.maximum(mx, s[:, lt])
            m = jnp.broadcast_to(jnp.max(mx, axis=1, keepdims=True), (2 * tq, LANES))
            ls = jnp.zeros((2 * tq, LANES), F32)
            for k0 in range(0, lk, LANES):
                p = jnp.exp2(sb_ref[:, k0:k0 + LANES] - m)
                ls = ls + p
                pb_ref[:, k0:k0 + LANES] = p.astype(BF16)
            l = jnp.sum(ls, axis=1, keepdims=True)
            on = jnp.dot(pb_ref[...], v_ref[0, :, cols], preferred_element_type=F32) / l
            o = on[0:tq] - lam * on[tq:2 * tq]
            ms = jnp.mean(o * o, axis=-1, keepdims=True)
            o = o * lax.rsqrt(ms + EPS) * g_ref[...] * (1.0 - lam_init)
            o_ref[0, q0:q0 + tq, cols] = o.astype(o_ref.dtype)


def diff_attention(q, k, v, lam_params, subln_g, lam_init, dh):
    b, lq, w = q.shape
    lk = k.shape[1]
    h = w // LANES
    tq = _tile(lq, 4 * ATTN_ROWS, SUBLANES)
    hb = _tile(h, 4, 1) if lk <= 2 * ATTN_KEYS else 1
    wb = hb * LANES
    return pl.pallas_call(
        functools.partial(_attn_kernel, lam_init=lam_init, dh=dh),
        grid=(b, h // hb, lq // tq),
        in_specs=[pl.BlockSpec((1, tq, wb), lambda bi, hi, qi: (bi, qi, hi)),
                  pl.BlockSpec((1, lk, wb), lambda bi, hi, qi: (bi, 0, hi)),
                  pl.BlockSpec((1, lk, wb), lambda bi, hi, qi: (bi, 0, hi)),
                  pl.BlockSpec((4, dh), lambda bi, hi, qi: (0, 0)),
                  pl.BlockSpec((1, LANES), lambda bi, hi, qi: (0, 0))],
        out_specs=pl.BlockSpec((1, tq, wb), lambda bi, hi, qi: (bi, qi, hi)),
        out_shape=jax.ShapeDtypeStruct((b, lq, w), BF16),
        scratch_shapes=[pltpu.VMEM((2, 2 * min(tq, ATTN_ROWS), lk), BF16),
                        pltpu.VMEM((2, 2 * min(tq, ATTN_ROWS), lk), F32)],
        compiler_params=_params("parallel", "parallel", "parallel"),
        name="diff_attention",
    )(q, k, v, lam_params, subln_g.astype(F32).reshape(1, LANES))


def _retention_kernel(q_ref, k_ref, v_ref, gate_ref, dx_ref, s0f_ref, s0b_ref,
                      o_ref, sf_ref, sb_ref, acc_ref, accb_ref, *, chunk, k_scale):
    seq = q_ref.shape[1]
    n_chunks = seq // chunk
    dx = dx_ref[0]
    lg = jnp.log1p(-jnp.exp2(-dx))
    lgf, lgb = lg[0:1], lg[1:2]
    ii = lax.broadcasted_iota(jnp.int32, (chunk, chunk), 0)
    jj = lax.broadcasted_iota(jnp.int32, (chunk, chunk), 1)
    d = (ii - jj).astype(F32)
    dmat = (jnp.where(d >= 0, jnp.exp(jnp.maximum(d, 0.0) * lgf[:, 0:1]), 0.0)
            + jnp.where(d <= 0, jnp.exp(jnp.maximum(-d, 0.0) * lgb[:, 0:1]), 0.0))
    pos = lax.broadcasted_iota(jnp.int32, (chunk, LANES), 0).astype(F32)
    qdec_f = jnp.exp((pos + 1.0) * lgf)
    kdec_f = jnp.exp((chunk - 1.0 - pos) * lgf)
    cdec_f = jnp.exp(chunk * lgf)
    qdec_b = jnp.exp((chunk - pos) * lgb)
    kdec_b = jnp.exp(pos * lgb)
    cdec_b = jnp.exp(chunk * lgb)
    dim0 = (((0,), (0,)), ((), ()))
    dim1 = (((1,), (1,)), ((), ()))

    def load(n):
        rows = pl.ds(pl.multiple_of(n * chunk, chunk), chunk)
        k = k_ref[0, rows, :].astype(F32)
        if k_scale != 1.0:
            k = k * k_scale
        return rows, q_ref[0, rows, :].astype(F32), k, v_ref[0, rows, :].astype(BF16)

    def step(n, carry):
        s_f, s_b = carry
        rows, q, k, v = load(n)
        att = lax.dot_general(q.astype(BF16), k.astype(BF16), dim1, preferred_element_type=F32) * dmat
        acc_ref[rows, :] = (jnp.dot(att.astype(BF16), v, preferred_element_type=F32)
                            + jnp.dot((q * qdec_f).astype(BF16), s_f.astype(BF16),
                                      preferred_element_type=F32))
        s_f = s_f * cdec_f + lax.dot_general((k * kdec_f).astype(BF16), v, dim0,
                                             preferred_element_type=F32)
        rows, q, k, v = load(n_chunks - 1 - n)
        accb_ref[rows, :] = jnp.dot((q * qdec_b).astype(BF16), s_b.astype(BF16),
                                    preferred_element_type=F32)
        s_b = s_b * cdec_b + lax.dot_general((k * kdec_b).astype(BF16), v, dim0,
                                             preferred_element_type=F32)
        return s_f, s_b

    sf_ref[0, 0], sb_ref[0, 0] = lax.fori_loop(0, n_chunks, step, (s0f_ref[0, 0], s0b_ref[0, 0]))

    o = acc_ref[...] + accb_ref[...]
    ms = jnp.mean(o * o, axis=-1, keepdims=True)
    gate = gate_ref[0].astype(F32)
    o_ref[0] = (o * lax.rsqrt(ms + EPS) * (gate * jax.nn.sigmoid(gate))).astype(o_ref.dtype)


def bi_retention(q, k, v, gate, k_scale, h, dexp, s0f, s0b):
    b, seq, _ = q[0].shape
    w = h * LANES
    chunk = _tile(seq, RET_CHUNK, LANES)
    head = lambda bi, hi: (bi, 0, hi)
    state = lambda bi, hi: (bi, hi, 0, 0)
    cols = lambda src: pl.BlockSpec((1, seq, LANES), lambda bi, hi, c0=src[1] // LANES: (bi, 0, c0 + hi))
    return pl.pallas_call(
        functools.partial(_retention_kernel, chunk=chunk, k_scale=k_scale),
        grid=(b, h),
        in_specs=[cols(q), cols(k), cols(v), cols(gate),
                  pl.BlockSpec((1, 2, LANES), lambda bi, hi: (hi, 0, 0)),
                  pl.BlockSpec((1, 1, LANES, LANES), state),
                  pl.BlockSpec((1, 1, LANES, LANES), state)],
        out_specs=[pl.BlockSpec((1, seq, LANES), head),
                   pl.BlockSpec((1, 1, LANES, LANES), state),
                   pl.BlockSpec((1, 1, LANES, LANES), state)],
        out_shape=[jax.ShapeDtypeStruct((b, seq, w), BF16),
                   jax.ShapeDtypeStruct((b, h, LANES, LANES), F32),
                   jax.ShapeDtypeStruct((b, h, LANES, LANES), F32)],
        scratch_shapes=[pltpu.VMEM((seq, LANES), F32), pltpu.VMEM((seq, LANES), F32)],
        compiler_params=_params("parallel", "parallel"),
        name="bi_retention",
    )(q[0], k[0], v[0], gate[0], dexp, s0f, s0b)


def _conv_ln_kernel(u_ref, prev_ref, next_ref, w_ref, g_ref, b_ref, o_ref, ext_ref, y_ref, shift_ref, *,
                    seq_len, taps, row_block):
    i = pl.program_id(1)
    c = pl.program_id(2)
    tm, cw = u_ref.shape[1], u_ref.shape[2]
    first = (i * tm) % seq_len == 0
    last = ((i + 1) * tm) % seq_len == 0
    zero = jnp.zeros((CONV_HALO, cw), F32)
    ext_ref[0:CONV_HALO, :] = jnp.where(first, zero, prev_ref[0])
    ext_ref[CONV_HALO:CONV_HALO + tm, :] = u_ref[0]
    ext_ref[CONV_HALO + tm:, :] = jnp.where(last, zero, next_ref[0])
    half = (taps - 1) // 2
    for r0 in range(0, tm, row_block):
        acc = None
        for b in range(SUBLANES):
            part = None
            for a in range(-((half + b) // SUBLANES), (half - b) // SUBLANES + 1):
                o = b + SUBLANES * a
                term = (ext_ref[pl.ds(CONV_HALO + r0 + SUBLANES * a, row_block + SUBLANES), :]
                        * w_ref[half + o:half + o + 1, :])
                part = term if part is None else part + term
            if b == 0:
                shifted = part[0:row_block]
            else:
                shift_ref[b % 2] = part
                shifted = shift_ref[b % 2, pl.ds(b, row_block), :]
            acc = shifted if acc is None else acc + shifted
        y_ref[c, pl.ds(r0, row_block), :] = acc

    @pl.when(c == pl.num_programs(2) - 1)
    def _():
        nc = y_ref.shape[0]
        d = nc * cw
        mu = sum(jnp.sum(y_ref[cc], axis=-1, keepdims=True) for cc in range(nc)) * (1.0 / d)
        var = sum(jnp.sum(jnp.square(y_ref[cc] - mu), axis=-1, keepdims=True)
                  for cc in range(nc)) * (1.0 / d)
        inv = lax.rsqrt(var + EPS)
        for cc in range(nc):
            cols = slice(cc * cw, (cc + 1) * cw)
            z = (y_ref[cc] - mu) * inv * g_ref[:, cols] + b_ref[:, cols]
            o_ref[0, :, cols] = (z * jax.nn.sigmoid(z)).astype(o_ref.dtype)


def conv_ln_silu(u, dw, ln_g, ln_b, seq_len):
    g, t, d = u.shape
    taps = dw.shape[0]
    half = (taps - 1) // 2
    assert taps % 2 == 1 and SUBLANES * (-(-half // SUBLANES)) <= CONV_HALO
    assert SUBLANES * (half // SUBLANES + 1) <= CONV_HALO
    tm = _tile(min(t, seq_len), 256, CONV_HALO)
    cw = _tile(d, 512, LANES)
    nh = tm // CONV_HALO
    n_halo_blocks = t // CONV_HALO
    w = jnp.zeros((2 * CONV_HALO, d), F32).at[:taps].set(dw.astype(F32))
    return pl.pallas_call(
        functools.partial(_conv_ln_kernel, seq_len=seq_len, taps=taps, row_block=min(tm, 32)),
        grid=(g, t // tm, d // cw),
        in_specs=[pl.BlockSpec((1, tm, cw), lambda b, i, c: (b, i, c)),
                  pl.BlockSpec((1, CONV_HALO, cw), lambda b, i, c: (b, jnp.maximum(i * nh - 1, 0), c)),
                  pl.BlockSpec((1, CONV_HALO, cw),
                               lambda b, i, c: (b, jnp.minimum((i + 1) * nh, n_halo_blocks - 1), c)),
                  pl.BlockSpec((2 * CONV_HALO, cw), lambda b, i, c: (0, c)),
                  pl.BlockSpec((1, d), lambda b, i, c: (0, 0)),
                  pl.BlockSpec((1, d), lambda b, i, c: (0, 0))],
        out_specs=pl.BlockSpec((1, tm, d), lambda b, i, c: (b, i, 0)),
        out_shape=jax.ShapeDtypeStruct((g, t, d), BF16),
        scratch_shapes=[pltpu.VMEM((tm + 2 * CONV_HALO, cw), F32),
                        pltpu.VMEM((d // cw, tm, cw), F32),
                        pltpu.VMEM((2, min(tm, 32) + SUBLANES, cw), F32)],
        compiler_params=_params("parallel", "parallel", "arbitrary"),
        name="conv_ln_silu",
    )(u, u, u, w, ln_g.astype(F32).reshape(1, d), ln_b.astype(F32).reshape(1, d))


FFN_HALO = 16


def _patch_rows(arr, patches):
    sub = lax.broadcasted_iota(jnp.int32, (SUBLANES, arr.shape[1]), 0)
    pieces, done = [], 0
    for r, val in patches:
        g0 = r // SUBLANES * SUBLANES
        assert g0 >= done
        if g0 > done:
            pieces.append(arr[done:g0])
        pieces.append(jnp.where(sub == r % SUBLANES, val, arr[g0:g0 + SUBLANES]))
        done = g0 + SUBLANES
    if done < arr.shape[0]:
        pieces.append(arr[done:])
    return jnp.concatenate(pieces, axis=0)


def _up_glu_kernel(h_ref, hp_ref, hn_ref, wg_ref, wv_ref, dg_ref, dv_ref, o_ref, hx_ref, *, seq_len):
    i = pl.program_id(1)
    tm = h_ref.shape[1]
    halo = FFN_HALO

    @pl.when(pl.program_id(2) == 0)
    def _():
        hx_ref[0:halo, :] = hp_ref[0]
        hx_ref[halo:halo + tm, :] = h_ref[0]
        hx_ref[halo + tm:, :] = hn_ref[0]

    cw = MXU_COLS if o_ref.shape[2] % MXU_COLS == 0 else LANES
    tile_in_one_seq = seq_len % tm == 0
    if tile_in_one_seq:
        first = lax.rem(i * tm, seq_len) == 0
        last = lax.rem((i + 1) * tm, seq_len) == 0

    def conv(w_ref, d_ref, cols):
        p = jnp.dot(hx_ref[...], w_ref[:, cols], preferred_element_type=F32)
        x = p[halo:halo + tm]
        if tile_in_one_seq:
            starts = [(0, jnp.where(first, 0.0, p[halo - 1:halo]))]
            ends = [(tm - 1, jnp.where(last, 0.0, p[halo + tm:halo + tm + 1]))]
        else:
            starts = [(r, 0.0) for r in range(0, tm, seq_len)]
            ends = [(r + seq_len - 1, 0.0) for r in range(0, tm, seq_len)]
        xm = _patch_rows(pltpu.roll(x, 1, axis=0), starts)
        xp = _patch_rows(pltpu.roll(x, tm - 1, axis=0), ends)
        d = d_ref[:, cols]
        return xm * d[0:1] + x * d[1:2] + xp * d[2:3]

    for c0 in range(0, o_ref.shape[2], cw):
        cols = slice(c0, c0 + cw)
        gate = conv(wg_ref, dg_ref, cols)
        val = conv(wv_ref, dv_ref, cols)
        o_ref[0, :, cols] = (gate * jax.nn.sigmoid(gate) * val).astype(o_ref.dtype)


def ffn_up_conv_glu(h, up, layer, dw, seq_len):
    g, t, d = h.shape
    f = up.shape[2] // 2
    assert dw.shape[0] == 3
    tm = _tile(t, 1024, FFN_HALO)
    assert seq_len % SUBLANES == 0 and (seq_len % tm == 0 or tm % seq_len == 0)
    tf = _tile(f, 512, LANES)
    nj = f // tf
    nh = tm // FFN_HALO
    n_halo_blocks = t // FFN_HALO
    w = jnp.zeros((SUBLANES, 2 * f), F32).at[:3].set(dw.astype(F32))
    return pl.pallas_call(
        functools.partial(_up_glu_kernel, seq_len=seq_len),
        grid=(g, t // tm, nj),
        in_specs=[pl.BlockSpec((1, tm, d), lambda b, i, j: (b, i, 0), pipeline_mode=pl.Buffered(1)),
                  pl.BlockSpec((1, FFN_HALO, d), lambda b, i, j: (b, jnp.maximum(i * nh - 1, 0), 0)),
                  pl.BlockSpec((1, FFN_HALO, d),
                               lambda b, i, j: (b, jnp.minimum((i + 1) * nh, n_halo_blocks - 1), 0)),
                  _w_spec(d, tf, layer),
                  _w_spec(d, tf, layer, nj),
                  pl.BlockSpec((SUBLANES, tf), lambda b, i, j: (0, j)),
                  pl.BlockSpec((SUBLANES, tf), lambda b, i, j: (0, j + nj))],
        out_specs=pl.BlockSpec((1, tm, tf), lambda b, i, j: (b, i, j)),
        out_shape=jax.ShapeDtypeStruct((g, t, f), BF16),
        scratch_shapes=[pltpu.VMEM((tm + 2 * FFN_HALO, d), BF16)],
        compiler_params=_params("parallel", "parallel", "arbitrary"),
        name="ffn_up_conv_glu",
    )(h, h, h, up, up, w, w)


def _mods(mods_l, rows, d):
    m = mods_l[rows[0]:rows[1]]
    return [m[:, k * d:(k + 1) * d].reshape(-1, 1, d) for k in range(6)]


def _ffn_block(x, mods, norm_g, up, dw, down, layer, seq_len):
    sh2, sc2, g2 = mods
    h = modulate(x, norm_g, sc2, sh2)
    a = ffn_up_conv_glu(h, up, layer, dw, seq_len)
    return matmul_residual([a], down, layer, x, g2)


def kernel(x_prompt, x_sample, cache_k_a, cache_v_a, state_ret_fwd, state_ret_bwd, c, c_ctx, adaln_w, adaln_b, norm1_g, norm2_g, w_in_ab, w_out_ab, q_norm_g, k_norm_g, lambda_q1, lambda_k1, lambda_q2, lambda_k2, subln_g, ret_decay_exp_fwd, ret_decay_exp_bwd, conv_pw1, conv_dw, conv_ln_g, conv_ln_b, conv_pw2, ffn_up, ffn_dw, ffn_down):
    batch, seq, d = x_prompt.shape
    dec_batch, dec_seq, _ = x_sample.shape
    depth = adaln_w.shape[0]
    _, _, past_len, h_a, dk_a = cache_k_a.shape
    dh_a = dk_a // 2
    dv_a = cache_v_a.shape[-1]
    _, _, h_b, dk_b, dv_b = state_ret_fwd.shape
    assert dk_a == LANES and dv_a == LANES and dk_b == LANES and dv_b == LANES
    assert dec_batch + 1 <= SUBLANES
    w_qa, w_va, w_qb, w_vb = h_a * dk_a, h_a * dv_a, h_b * dk_b, h_b * dv_b
    col = {}
    off = 0
    for name, width in (("qa", w_qa), ("ka", w_qa), ("va", w_va), ("qb", w_qb), ("kb", w_qb),
                        ("vb", w_vb), ("gb", w_vb)):
        col[name] = off
        off += width

    conds = jnp.zeros((SUBLANES, d), F32).at[0].set(c_ctx).at[1:1 + dec_batch].set(c)
    mods_all = adaln(conds, adaln_w, adaln_b)

    w_in_bf, w_out_bf = w_in_ab.astype(BF16), w_out_ab.astype(BF16)
    pw1_bf, pw2_bf = conv_pw1.astype(BF16), conv_pw2.astype(BF16)
    up_bf, down_bf = ffn_up.astype(BF16), ffn_down.astype(BF16)

    rope_a = rope_tables(dec_seq, dh_a)
    rope_b = rope_tables(dec_seq, dk_b)

    def run(x, rows, seq_len, latent):
        g, t, _ = x.shape
        n_seq = g * t // seq_len
        new_ctx = []
        for l in range(depth):
            sh1, sc1, g1, sh2, sc2, g2 = _mods(mods_all[l], rows, d)
            h = modulate(x, norm1_g[l], sc1, sh1)
            i = l // 2
            if l % 2 == 0:
                proj = matmul(h, w_in_bf, i, F32)
                ra = rope_a if latent else None
                qa = head_prep(proj, col["qa"], w_qa, gain=q_norm_g[i], group=dh_a, rope=ra,
                               postscale=dh_a ** -0.5 * math.log2(math.e))
                ka = head_prep(proj, col["ka"], w_qa, gain=k_norm_g[i], group=dh_a, rope=ra,
                               want_f32=not latent)
                va = head_prep(proj, col["va"], w_va, want_f32=not latent)
                if latent:
                    qb = head_prep(proj, col["qb"], w_qb, rope=rope_b)
                    kb = head_prep(proj, col["kb"], w_qb, rope=rope_b, prescale=dk_b ** -0.5)
                seqs = lambda z: z.reshape(n_seq, seq_len, z.shape[-1])
                if latent:
                    k_all = jnp.concatenate(
                        [cache_k_a[:, i].reshape(dec_batch, past_len, w_qa).astype(BF16), ka], axis=1)
                    v_all = jnp.concatenate(
                        [cache_v_a[:, i].reshape(dec_batch, past_len, w_va).astype(BF16), va], axis=1)
                    s0f, s0b = state_ret_fwd[:, i], state_ret_bwd[:, i]
                else:
                    ka, ka_f32 = ka
                    va, va_f32 = va
                    k_all, v_all = seqs(ka), seqs(va)
                    s0f = jnp.zeros((n_seq, h_b, dk_b, dv_b), F32)
                    s0b = s0f
                lam_init = 0.8 - 0.6 * math.exp(-0.3 * l)
                lam_params = jnp.stack([lambda_q1[i], lambda_k1[i], lambda_q2[i], lambda_k2[i]]).astype(F32)
                oa = diff_attention(seqs(qa), k_all, v_all, lam_params, subln_g[i], lam_init, dh_a)
                dexp = jnp.stack([ret_decay_exp_fwd[i], ret_decay_exp_bwd[i]], axis=1).astype(F32)
                dexp = jnp.broadcast_to(dexp[:, :, None], (h_b, 2, LANES))
                projs = seqs(proj)
                if latent:
                    qkv = ((seqs(qb), 0), (seqs(kb), 0), (projs, col["vb"]))
                    k_scale = 1.0
                else:
                    qkv = ((projs, col["qb"]), (projs, col["kb"]), (projs, col["vb"]))
                    k_scale = dk_b ** -0.5
                ob, s_f, s_b = bi_retention(*qkv, (projs, col["gb"]), k_scale, h_b, dexp, s0f, s0b)
                x = matmul_residual([oa.reshape(g, t, w_va), ob.reshape(g, t, w_vb)], w_out_bf, i, x, g1)
                if not latent:
                    new_ctx.append((ka_f32.reshape(n_seq, seq_len, h_a, dk_a),
                                    va_f32.reshape(n_seq, seq_len, h_a, dv_a), s_f, s_b))
            else:
                u = matmul_glu(h, pw1_bf, i, F32)
                z = conv_ln_silu(u, conv_dw[i], conv_ln_g[i], conv_ln_b[i], seq_len)
                x = matmul_residual([z], pw2_bf, i, x, g1)
            x = _ffn_block(x, (sh2, sc2, g2), norm2_g[l], up_bf, ffn_dw[l], down_bf, l, seq_len)
        return x, new_ctx

    y_ctx, new_ctx = run(x_prompt.reshape(1, batch * seq, d), (0, 1), seq, False)
    y_lat, _ = run(x_sample, (1, 1 + dec_batch), dec_seq, True)
    stack = lambda k: jnp.stack([n[k] for n in new_ctx], axis=1)
    return (y_ctx.reshape(batch, seq, d), y_lat, stack(0), stack(1), stack(2), stack(3))
```

```python
import functools
import math

import jax
import jax.numpy as jnp
from jax import lax
from jax.experimental import pallas as pl
from jax.experimental.pallas import tpu as pltpu

F32 = jnp.float32
BF16 = jnp.bfloat16
EPS = 1e-6
GRID_W = 64
ROPE_BASE = 10000.0
LANES = 128
SUBLANES = 8
MXU_COLS = 256
CONV_HALO = 16
CONV_ROWS = 64
CONV_COLS = 256
VMEM_LIMIT_V7X = 56 * 1024 * 1024
RET_CHUNK = 256


def _tile(dim, pref, align):
    if dim <= pref:
        return dim
    t = (pref // align) * align
    while t >= align:
        if dim % t == 0:
            return t
        t -= align
    return dim


def _params(*sem):
    return pltpu.CompilerParams(dimension_semantics=sem, vmem_limit_bytes=VMEM_LIMIT_V7X)


def _adaln_kernel(c_ref, w_ref, b_ref, o_ref):
    c = c_ref[...]
    s = (c * jax.nn.sigmoid(c)).astype(BF16)
    w = w_ref[0].astype(BF16)
    o_ref[0] = jnp.dot(s, w, preferred_element_type=F32) + b_ref[0]


def adaln(conds, w, b):
    depth, d, n = w.shape
    tn = _tile(n, 512, LANES)
    return pl.pallas_call(
        _adaln_kernel,
        grid=(depth, n // tn),
        in_specs=[pl.BlockSpec((SUBLANES, d), lambda l, j: (0, 0)),
                  pl.BlockSpec((1, d, tn), lambda l, j: (l, 0, j)),
                  pl.BlockSpec((1, 1, tn), lambda l, j: (l, 0, j))],
        out_specs=pl.BlockSpec((1, SUBLANES, tn), lambda l, j: (l, 0, j)),
        out_shape=jax.ShapeDtypeStruct((depth, SUBLANES, n), F32),
        compiler_params=_params("parallel", "parallel"),
        name="adaln",
    )(conds, w, b.reshape(depth, 1, n))


def _modulate_kernel(x_ref, g_ref, sc_ref, sh_ref, o_ref):
    d = x_ref.shape[2]
    cw = _tile(d, 512, LANES)
    chunks = [slice(c0, c0 + cw) for c0 in range(0, d, cw)]
    ss = sum(jnp.sum(jnp.square(x_ref[0, :, cs]), axis=-1, keepdims=True) for cs in chunks)
    inv = lax.rsqrt(ss * (1.0 / d) + EPS)
    for cs in chunks:
        y = x_ref[0, :, cs] * inv * g_ref[:, cs]
        o_ref[0, :, cs] = (y * (1.0 + sc_ref[0, :, cs]) + sh_ref[0, :, cs]).astype(o_ref.dtype)


def modulate(x, gain, scale, shift):
    g, t, d = x.shape
    tm = _tile(t, 256, SUBLANES)
    return pl.pallas_call(
        _modulate_kernel,
        grid=(g, t // tm),
        in_specs=[pl.BlockSpec((1, tm, d), lambda b, i: (b, i, 0)),
                  pl.BlockSpec((1, d), lambda b, i: (0, 0)),
                  pl.BlockSpec((1, 1, d), lambda b, i: (b, 0, 0)),
                  pl.BlockSpec((1, 1, d), lambda b, i: (b, 0, 0))],
        out_specs=pl.BlockSpec((1, tm, d), lambda b, i: (b, i, 0)),
        out_shape=jax.ShapeDtypeStruct((g, t, d), BF16),
        compiler_params=_params("parallel", "parallel"),
        name="modulate",
    )(x, gain.reshape(1, d), scale, shift)


def _mm_kernel(a_ref, w_ref, o_ref):
    o_ref[0] = jnp.dot(a_ref[0], w_ref[...], preferred_element_type=F32).astype(o_ref.dtype)


def _mm_res_kernel(*refs):
    *a_refs, w_ref, r_ref, g_ref, o_ref = refs
    acc, k0 = None, 0
    for a_ref in a_refs:
        k1 = k0 + a_ref.shape[2]
        part = jnp.dot(a_ref[0], w_ref[k0:k1, :], preferred_element_type=F32)
        acc = part if acc is None else acc + part
        k0 = k1
    o_ref[0] = r_ref[0] + g_ref[0] * acc


def _mm_glu_kernel(a_ref, wa_ref, wg_ref, o_ref):
    a = jnp.dot(a_ref[0], wa_ref[...], preferred_element_type=F32)
    g = jnp.dot(a_ref[0], wg_ref[...], preferred_element_type=F32)
    o_ref[0] = (a * jax.nn.sigmoid(g)).astype(o_ref.dtype)


def _mm_tiles(t, k, n):
    tm = _tile(t, 1024, SUBLANES)
    tn = _tile(n, 512 if k <= 8192 else 256, LANES)
    return tm, tn


def _lhs_spec(tm, k_piece, k_total):
    mode = pl.Buffered(1) if k_total > 8192 else None
    return pl.BlockSpec((1, tm, k_piece), lambda b, i, j: (b, i, 0), pipeline_mode=mode)


def _w_spec(k, tn, layer, col_block_offset=0):
    return pl.BlockSpec((None, k, tn), lambda b, i, j: (layer, 0, j + col_block_offset))


def matmul(a, w, layer, out_dtype):
    g, t, k = a.shape
    n = w.shape[2]
    tm, tn = _mm_tiles(t, k, n)
    return pl.pallas_call(
        _mm_kernel,
        grid=(g, t // tm, n // tn),
        in_specs=[pl.BlockSpec((1, tm, k), lambda b, i, j: (b, i, 0)),
                  _w_spec(k, tn, layer)],
        out_specs=pl.BlockSpec((1, tm, tn), lambda b, i, j: (b, i, j)),
        out_shape=jax.ShapeDtypeStruct((g, t, n), out_dtype),
        compiler_params=_params("parallel", "parallel", "parallel"),
        name="matmul",
    )(a, w)


def matmul_residual(a_pieces, w, layer, res, gate):
    g, t, _ = a_pieces[0].shape
    k = sum(a.shape[2] for a in a_pieces)
    n = w.shape[2]
    assert k == w.shape[1]
    tm, tn = _mm_tiles(t, k, n)
    return pl.pallas_call(
        _mm_res_kernel,
        grid=(g, t // tm, n // tn),
        in_specs=[_lhs_spec(tm, a.shape[2], k) for a in a_pieces]
                 + [_w_spec(k, tn, layer),
                    pl.BlockSpec((1, tm, tn), lambda b, i, j: (b, i, j)),
                    pl.BlockSpec((1, 1, tn), lambda b, i, j: (b, 0, j))],
        out_specs=pl.BlockSpec((1, tm, tn), lambda b, i, j: (b, i, j)),
        out_shape=jax.ShapeDtypeStruct((g, t, n), F32),
        compiler_params=_params("parallel", "parallel", "parallel"),
        name="matmul_residual",
    )(*a_pieces, w, res, gate)


def matmul_glu(a, w, layer, out_dtype):
    g, t, k = a.shape
    n = w.shape[2] // 2
    tm, tn = _mm_tiles(t, k, n)
    nj = n // tn
    return pl.pallas_call(
        _mm_glu_kernel,
        grid=(g, t // tm, nj),
        in_specs=[pl.BlockSpec((1, tm, k), lambda b, i, j: (b, i, 0)),
                  _w_spec(k, tn, layer),
                  _w_spec(k, tn, layer, nj)],
        out_specs=pl.BlockSpec((1, tm, tn), lambda b, i, j: (b, i, j)),
        out_shape=jax.ShapeDtypeStruct((g, t, n), out_dtype),
        compiler_params=_params("parallel", "parallel", "parallel"),
        name="matmul_glu",
    )(a, w, w)


def _prep_kernel(*refs, norm, rope_q, prescale, postscale, want_f32, group):
    refs = list(refs)
    x_ref = refs.pop(0)
    if norm:
        g_ref = refs.pop(0)
        bd_ref = refs.pop(0)
    if rope_q:
        cos_ref = refs.pop(0)
        sa_ref = refs.pop(0)
        sb_ref = refs.pop(0)
    o_ref = refs.pop(0)
    f_ref = refs.pop(0) if want_f32 else None
    for hh in range(x_ref.shape[2] // LANES):
        cols = slice(hh * LANES, (hh + 1) * LANES)
        y = x_ref[0, :, cols].astype(F32)
        if norm:
            sq = y * y
            hi = sq.astype(BF16)
            lo = (sq - hi.astype(F32)).astype(BF16)
            ss = (jnp.dot(hi, bd_ref[...], preferred_element_type=F32)
                  + jnp.dot(lo, bd_ref[...], preferred_element_type=F32))
            y = y * lax.rsqrt(ss * (1.0 / group) + EPS) * g_ref[...]
        if prescale != 1.0:
            y = y * prescale
        if rope_q:
            y = (y * cos_ref[...]
                 + pltpu.roll(y, LANES - rope_q, axis=1) * sa_ref[...]
                 + pltpu.roll(y, rope_q, axis=1) * sb_ref[...])
        if want_f32:
            f_ref[0, :, cols] = y
        if postscale != 1.0:
            y = y * postscale
        o_ref[0, :, cols] = y.astype(o_ref.dtype)


def head_prep(proj, col_start, width, *, gain=None, group=None, rope=None, prescale=1.0,
              postscale=1.0, want_f32=False):
    g, t, _ = proj.shape
    cw = _tile(width, 1024, LANES)
    tm = _tile(t, 512, SUBLANES)
    c0 = col_start // cw
    assert col_start % cw == 0
    in_specs = [pl.BlockSpec((1, tm, cw), lambda b, i, j: (b, i, c0 + j))]
    args = [proj]
    if gain is not None:
        lane = jnp.arange(LANES)
        bd = (lane[:, None] // group == lane[None, :] // group).astype(BF16)
        in_specs += [pl.BlockSpec((1, LANES), lambda b, i, j: (0, 0)),
                     pl.BlockSpec((LANES, LANES), lambda b, i, j: (0, 0))]
        args += [jnp.tile(gain.astype(F32), LANES // group).reshape(1, LANES), bd]
    rope_q = 0
    if rope is not None:
        rope_q, cos, sin_a, sin_b = rope
        in_specs += [pl.BlockSpec((tm, LANES), lambda b, i, j: (i, 0))] * 3
        args += [cos, sin_a, sin_b]
    out_specs = [pl.BlockSpec((1, tm, cw), lambda b, i, j: (b, i, j))]
    out_shape = [jax.ShapeDtypeStruct((g, t, width), BF16)]
    if want_f32:
        out_specs.append(pl.BlockSpec((1, tm, cw), lambda b, i, j: (b, i, j)))
        out_shape.append(jax.ShapeDtypeStruct((g, t, width), F32))
    outs = pl.pallas_call(
        functools.partial(_prep_kernel, norm=gain is not None, rope_q=rope_q, prescale=prescale,
                          postscale=postscale, want_f32=want_f32, group=group),
        grid=(g, t // tm, width // cw),
        in_specs=in_specs,
        out_specs=out_specs,
        out_shape=out_shape,
        compiler_params=_params("parallel", "parallel", "parallel"),
        name="head_prep",
    )(*args)
    return outs if want_f32 else outs[0]


def rope_tables(n_tokens, dim):
    rows = n_tokens // GRID_W
    row = jnp.repeat(jnp.arange(rows, dtype=F32), GRID_W)
    col = jnp.tile(jnp.arange(GRID_W, dtype=F32), rows)
    n_freq = dim // 4
    inv = ROPE_BASE ** (-jnp.arange(n_freq, dtype=F32) / n_freq)
    ar = row[:, None] * inv
    ac = col[:, None] * inv
    ang = jnp.concatenate([ar, ar, ac, ac], axis=-1)
    cos, sin = jnp.cos(ang), jnp.sin(ang)
    even = ((jnp.arange(dim) // n_freq) % 2 == 0)[None, :]
    sin_a = jnp.where(even, -sin, 0.0)
    sin_b = jnp.where(even, 0.0, sin)
    rep = LANES // dim
    return n_freq, jnp.tile(cos, (1, rep)), jnp.tile(sin_a, (1, rep)), jnp.tile(sin_b, (1, rep))


ATTN_ROWS = 128
ATTN_KEYS = 256


def _attn_kernel(q_ref, k_ref, v_ref, lp_ref, g_ref, o_ref, p_ref, s_ref, *, lam_init, dh):
    lk = k_ref.shape[1]
    tq = min(q_ref.shape[1], ATTN_ROWS)
    tk = _tile(lk, ATTN_KEYS, LANES)
    lp = lp_ref[...]
    lam = (jnp.exp(jnp.sum(lp[0:1] * lp[1:2], axis=1, keepdims=True))
           - jnp.exp(jnp.sum(lp[2:3] * lp[3:4], axis=1, keepdims=True)) + lam_init)
    dim1 = (((1,), (1,)), ((), ()))
    lane_tiles = [slice(j, j + LANES) for j in range(0, tk, LANES)]
    tile = 0
    for hh in range(q_ref.shape[2] // LANES):
        cols = slice(hh * LANES, (hh + 1) * LANES)
        for q0 in range(0, q_ref.shape[1], tq):
            pb_ref = p_ref.at[tile % 2]
            sb_ref = s_ref.at[tile % 2]
            tile += 1
            q = q_ref[0, q0:q0 + tq, cols]
            lane = lax.broadcasted_iota(jnp.int32, q.shape, 1)
            zero = jnp.zeros_like(q)
            qs = jnp.concatenate([jnp.where(lane < dh, q, zero), jnp.where(lane >= dh, q, zero)], axis=0)
            mx = jnp.full((2 * tq, LANES), -jnp.inf, F32)
            for k0 in range(0, lk, tk):
                s = lax.dot_general(qs, k_ref[0, k0:k0 + tk, cols], dim1, preferred_element_type=F32)
                sb_ref[:, k0:k0 + tk] = s
                for lt in lane_tiles:
                    mx = jnp.maximum(mx, s[:, lt])
            m = jnp.broadcast_to(jnp.max(mx, axis=1, keepdims=True), (2 * tq, LANES))
            ls = jnp.zeros((2 * tq, LANES), F32)
            for k0 in range(0, lk, LANES):
                p = jnp.exp2(sb_ref[:, k0:k0 + LANES] - m)
                ls = ls + p
                pb_ref[:, k0:k0 + LANES] = p.astype(BF16)
            l = jnp.sum(ls, axis=1, keepdims=True)
            on = jnp.dot(pb_ref[...], v_ref[0, :, cols], preferred_element_type=F32) / l
            o = on[0:tq] - lam * on[tq:2 * tq]
            ms = jnp.mean(o * o, axis=-1, keepdims=True)
            o = o * lax.rsqrt(ms + EPS) * g_ref[...] * (1.0 - lam_init)
            o_ref[0, q0:q0 + tq, cols] = o.astype(o_ref.dtype)


def diff_attention(q, k, v, lam_params, subln_g, lam_init, dh):
    b, lq, w = q.shape
    lk = k.shape[1]
    h = w // LANES
    tq = _tile(lq, 4 * ATTN_ROWS, SUBLANES)
    hb = _tile(h, 4, 1) if lk <= 2 * ATTN_KEYS else 1
    wb = hb * LANES
    return pl.pallas_call(
        functools.partial(_attn_kernel, lam_init=lam_init, dh=dh),
        grid=(b, h // hb, lq // tq),
        in_specs=[pl.BlockSpec((1, tq, wb), lambda bi, hi, qi: (bi, qi, hi)),
                  pl.BlockSpec((1, lk, wb), lambda bi, hi, qi: (bi, 0, hi)),
                  pl.BlockSpec((1, lk, wb), lambda bi, hi, qi: (bi, 0, hi)),
                  pl.BlockSpec((4, dh), lambda bi, hi, qi: (0, 0)),
                  pl.BlockSpec((1, LANES), lambda bi, hi, qi: (0, 0))],
        out_specs=pl.BlockSpec((1, tq, wb), lambda bi, hi, qi: (bi, qi, hi)),
        out_shape=jax.ShapeDtypeStruct((b, lq, w), BF16),
        scratch_shapes=[pltpu.VMEM((2, 2 * min(tq, ATTN_ROWS), lk), BF16),
                        pltpu.VMEM((2, 2 * min(tq, ATTN_ROWS), lk), F32)],
        compiler_params=_params("parallel", "parallel", "parallel"),
        name="diff_attention",
    )(q, k, v, lam_params, subln_g.astype(F32).reshape(1, LANES))


def _retention_kernel(q_ref, k_ref, v_ref, gate_ref, dx_ref, s0f_ref, s0b_ref,
                      o_ref, sf_ref, sb_ref, acc_ref, accb_ref, *, chunk, k_scale):
    seq = q_ref.shape[1]
    n_chunks = seq // chunk
    dx = dx_ref[0]
    lg = jnp.log1p(-jnp.exp2(-dx))
    lgf, lgb = lg[0:1], lg[1:2]
    ii = lax.broadcasted_iota(jnp.int32, (chunk, chunk), 0)
    jj = lax.broadcasted_iota(jnp.int32, (chunk, chunk), 1)
    d = (ii - jj).astype(F32)
    dmat = (jnp.where(d >= 0, jnp.exp(jnp.maximum(d, 0.0) * lgf[:, 0:1]), 0.0)
            + jnp.where(d <= 0, jnp.exp(jnp.maximum(-d, 0.0) * lgb[:, 0:1]), 0.0))
    pos = lax.broadcasted_iota(jnp.int32, (chunk, LANES), 0).astype(F32)
    qdec_f = jnp.exp((pos + 1.0) * lgf)
    kdec_f = jnp.exp((chunk - 1.0 - pos) * lgf)
    cdec_f = jnp.exp(chunk * lgf)
    qdec_b = jnp.exp((chunk - pos) * lgb)
    kdec_b = jnp.exp(pos * lgb)
    cdec_b = jnp.exp(chunk * lgb)
    dim0 = (((0,), (0,)), ((), ()))
    dim1 = (((1,), (1,)), ((), ()))

    def load(n):
        rows = pl.ds(pl.multiple_of(n * chunk, chunk), chunk)
        k = k_ref[0, rows, :].astype(F32)
        if k_scale != 1.0:
            k = k * k_scale
        return rows, q_ref[0, rows, :].astype(F32), k, v_ref[0, rows, :].astype(BF16)

    def step(n, carry):
        s_f, s_b = carry
        rows, q, k, v = load(n)
        att = lax.dot_general(q.astype(BF16), k.astype(BF16), dim1, preferred_element_type=F32) * dmat
        acc_ref[rows, :] = (jnp.dot(att.astype(BF16), v, preferred_element_type=F32)
                            + jnp.dot((q * qdec_f).astype(BF16), s_f.astype(BF16),
                                      preferred_element_type=F32))
        s_f = s_f * cdec_f + lax.dot_general((k * kdec_f).astype(BF16), v, dim0,
                                             preferred_element_type=F32)
        rows, q, k, v = load(n_chunks - 1 - n)
        accb_ref[rows, :] = jnp.dot((q * qdec_b).astype(BF16), s_b.astype(BF16),
                                    preferred_element_type=F32)
        s_b = s_b * cdec_b + lax.dot_general((k * kdec_b).astype(BF16), v, dim0,
                                             preferred_element_type=F32)
        return s_f, s_b

    sf_ref[0, 0], sb_ref[0, 0] = lax.fori_loop(0, n_chunks, step, (s0f_ref[0, 0], s0b_ref[0, 0]))

    o = acc_ref[...] + accb_ref[...]
    ms = jnp.mean(o * o, axis=-1, keepdims=True)
    gate = gate_ref[0].astype(F32)
    o_ref[0] = (o * lax.rsqrt(ms + EPS) * (gate * jax.nn.sigmoid(gate))).astype(o_ref.dtype)


def bi_retention(q, k, v, gate, k_scale, h, dexp, s0f, s0b):
    b, seq, _ = q[0].shape
    w = h * LANES
    chunk = _tile(seq, RET_CHUNK, LANES)
    head = lambda bi, hi: (bi, 0, hi)
    state = lambda bi, hi: (bi, hi, 0, 0)
    cols = lambda src: pl.BlockSpec((1, seq, LANES), lambda bi, hi, c0=src[1] // LANES: (bi, 0, c0 + hi))
    return pl.pallas_call(
        functools.partial(_retention_kernel, chunk=chunk, k_scale=k_scale),
        grid=(b, h),
        in_specs=[cols(q), cols(k), cols(v), cols(gate),
                  pl.BlockSpec((1, 2, LANES), lambda bi, hi: (hi, 0, 0)),
                  pl.BlockSpec((1, 1, LANES, LANES), state),
                  pl.BlockSpec((1, 1, LANES, LANES), state)],
        out_specs=[pl.BlockSpec((1, seq, LANES), head),
                   pl.BlockSpec((1, 1, LANES, LANES), state),
                   pl.BlockSpec((1, 1, LANES, LANES), state)],
        out_shape=[jax.ShapeDtypeStruct((b, seq, w), BF16),
                   jax.ShapeDtypeStruct((b, h, LANES, LANES), F32),
                   jax.ShapeDtypeStruct((b, h, LANES, LANES), F32)],
        scratch_shapes=[pltpu.VMEM((seq, LANES), F32), pltpu.VMEM((seq, LANES), F32)],
        compiler_params=_params("parallel", "parallel"),
        name="bi_retention",
    )(q[0], k[0], v[0], gate[0], dexp, s0f, s0b)


def _conv_ln_kernel(u_ref, prev_ref, next_ref, w_ref, g_ref, b_ref, o_ref, ext_ref, y_ref, shift_ref, *,
                    seq_len, taps, row_block):
    i = pl.program_id(1)
    c = pl.program_id(2)
    tm, cw = u_ref.shape[1], u_ref.shape[2]
    first = (i * tm) % seq_len == 0
    last = ((i + 1) * tm) % seq_len == 0
    zero = jnp.zeros((CONV_HALO, cw), F32)
    ext_ref[0:CONV_HALO, :] = jnp.where(first, zero, prev_ref[0])
    ext_ref[CONV_HALO:CONV_HALO + tm, :] = u_ref[0]
    ext_ref[CONV_HALO + tm:, :] = jnp.where(last, zero, next_ref[0])
    half = (taps - 1) // 2
    sw = shift_ref.shape[2]
    for c0 in range(0, cw, sw):
        cols = slice(c0, c0 + sw)
        for r0 in range(0, tm, row_block):
            acc = None
            for b in range(SUBLANES):
                part = None
                for a in range(-((half + b) // SUBLANES), (half - b) // SUBLANES + 1):
                    o = b + SUBLANES * a
                    term = (ext_ref[pl.ds(CONV_HALO + r0 + SUBLANES * a, row_block + SUBLANES), cols]
                            * w_ref[half + o:half + o + 1, cols])
                    part = term if part is None else part + term
                if b == 0:
                    shifted = part[0:row_block]
                else:
                    shift_ref[b % 2] = part
                    shifted = shift_ref[b % 2, pl.ds(b, row_block), :]
                acc = shifted if acc is None else acc + shifted
            y_ref[c, pl.ds(r0, row_block), cols] = acc

    @pl.when(c == pl.num_programs(2) - 1)
    def _():
        nc = y_ref.shape[0]
        d = nc * cw
        mu = sum(jnp.sum(y_ref[cc], axis=-1, keepdims=True) for cc in range(nc)) * (1.0 / d)
        var = sum(jnp.sum(jnp.square(y_ref[cc] - mu), axis=-1, keepdims=True)
                  for cc in range(nc)) * (1.0 / d)
        inv = lax.rsqrt(var + EPS)
        for cc in range(nc):
            cols = slice(cc * cw, (cc + 1) * cw)
            z = (y_ref[cc] - mu) * inv * g_ref[:, cols] + b_ref[:, cols]
            o_ref[0, :, cols] = (z * jax.nn.sigmoid(z)).astype(o_ref.dtype)


def conv_ln_silu(u, dw, ln_g, ln_b, seq_len):
    g, t, d = u.shape
    taps = dw.shape[0]
    half = (taps - 1) // 2
    assert taps % 2 == 1 and SUBLANES * (-(-half // SUBLANES)) <= CONV_HALO
    assert SUBLANES * (half // SUBLANES + 1) <= CONV_HALO
    tm = _tile(min(t, seq_len), 256, CONV_HALO)
    cw = _tile(d, 512, LANES)
    row_block = min(tm, CONV_ROWS)
    sw = _tile(cw, CONV_COLS, LANES)
    nh = tm // CONV_HALO
    n_halo_blocks = t // CONV_HALO
    w = jnp.zeros((2 * CONV_HALO, d), F32).at[:taps].set(dw.astype(F32))
    return pl.pallas_call(
        functools.partial(_conv_ln_kernel, seq_len=seq_len, taps=taps, row_block=row_block),
        grid=(g, t // tm, d // cw),
        in_specs=[pl.BlockSpec((1, tm, cw), lambda b, i, c: (b, i, c)),
                  pl.BlockSpec((1, CONV_HALO, cw), lambda b, i, c: (b, jnp.maximum(i * nh - 1, 0), c)),
                  pl.BlockSpec((1, CONV_HALO, cw),
                               lambda b, i, c: (b, jnp.minimum((i + 1) * nh, n_halo_blocks - 1), c)),
                  pl.BlockSpec((2 * CONV_HALO, cw), lambda b, i, c: (0, c)),
                  pl.BlockSpec((1, d), lambda b, i, c: (0, 0)),
                  pl.BlockSpec((1, d), lambda b, i, c: (0, 0))],
        out_specs=pl.BlockSpec((1, tm, d), lambda b, i, c: (b, i, 0)),
        out_shape=jax.ShapeDtypeStruct((g, t, d), BF16),
        scratch_shapes=[pltpu.VMEM((tm + 2 * CONV_HALO, cw), F32),
                        pltpu.VMEM((d // cw, tm, cw), F32),
                        pltpu.VMEM((2, row_block + SUBLANES, sw), F32)],
        compiler_params=_params("parallel", "parallel", "arbitrary"),
        name="conv_ln_silu",
    )(u, u, u, w, ln_g.astype(F32).reshape(1, d), ln_b.astype(F32).reshape(1, d))


FFN_HALO = 16


def _patch_rows(arr, patches):
    sub = lax.broadcasted_iota(jnp.int32, (SUBLANES, arr.shape[1]), 0)
    pieces, done = [], 0
    for r, val in patches:
        g0 = r // SUBLANES * SUBLANES
        assert g0 >= done
        if g0 > done:
            pieces.append(arr[done:g0])
        pieces.append(jnp.where(sub == r % SUBLANES, val, arr[g0:g0 + SUBLANES]))
        done = g0 + SUBLANES
    if done < arr.shape[0]:
        pieces.append(arr[done:])
    return jnp.concatenate(pieces, axis=0)


def _up_glu_kernel(h_ref, hp_ref, hn_ref, wg_ref, wv_ref, dg_ref, dv_ref, o_ref, hx_ref, *, seq_len):
    i = pl.program_id(1)
    tm = h_ref.shape[1]
    halo = FFN_HALO

    @pl.when(pl.program_id(2) == 0)
    def _():
        hx_ref[0:halo, :] = hp_ref[0]
        hx_ref[halo:halo + tm, :] = h_ref[0]
        hx_ref[halo + tm:, :] = hn_ref[0]

    cw = MXU_COLS if o_ref.shape[2] % MXU_COLS == 0 else LANES
    tile_in_one_seq = seq_len % tm == 0
    if tile_in_one_seq:
        first = lax.rem(i * tm, seq_len) == 0
        last = lax.rem((i + 1) * tm, seq_len) == 0

    def conv(w_ref, d_ref, cols):
        p = jnp.dot(hx_ref[...], w_ref[:, cols], preferred_element_type=F32)
        x = p[halo:halo + tm]
        if tile_in_one_seq:
            starts = [(0, jnp.where(first, 0.0, p[halo - 1:halo]))]
            ends = [(tm - 1, jnp.where(last, 0.0, p[halo + tm:halo + tm + 1]))]
        else:
            starts = [(r, 0.0) for r in range(0, tm, seq_len)]
            ends = [(r + seq_len - 1, 0.0) for r in range(0, tm, seq_len)]
        xm = _patch_rows(pltpu.roll(x, 1, axis=0), starts)
        xp = _patch_rows(pltpu.roll(x, tm - 1, axis=0), ends)
        d = d_ref[:, cols]
        return xm * d[0:1] + x * d[1:2] + xp * d[2:3]

    for c0 in range(0, o_ref.shape[2], cw):
        cols = slice(c0, c0 + cw)
        gate = conv(wg_ref, dg_ref, cols)
        val = conv(wv_ref, dv_ref, cols)
        o_ref[0, :, cols] = (gate * jax.nn.sigmoid(gate) * val).astype(o_ref.dtype)


def ffn_up_conv_glu(h, up, layer, dw, seq_len):
    g, t, d = h.shape
    f = up.shape[2] // 2
    assert dw.shape[0] == 3
    tm = _tile(t, 1024, FFN_HALO)
    assert seq_len % SUBLANES == 0 and (seq_len % tm == 0 or tm % seq_len == 0)
    tf = _tile(f, 512, LANES)
    nj = f // tf
    nh = tm // FFN_HALO
    n_halo_blocks = t // FFN_HALO
    w = jnp.zeros((SUBLANES, 2 * f), F32).at[:3].set(dw.astype(F32))
    return pl.pallas_call(
        functools.partial(_up_glu_kernel, seq_len=seq_len),
        grid=(g, t // tm, nj),
        in_specs=[pl.BlockSpec((1, tm, d), lambda b, i, j: (b, i, 0), pipeline_mode=pl.Buffered(1)),
                  pl.BlockSpec((1, FFN_HALO, d), lambda b, i, j: (b, jnp.maximum(i * nh - 1, 0), 0)),
                  pl.BlockSpec((1, FFN_HALO, d),
                               lambda b, i, j: (b, jnp.minimum((i + 1) * nh, n_halo_blocks - 1), 0)),
                  _w_spec(d, tf, layer),
                  _w_spec(d, tf, layer, nj),
                  pl.BlockSpec((SUBLANES, tf), lambda b, i, j: (0, j)),
                  pl.BlockSpec((SUBLANES, tf), lambda b, i, j: (0, j + nj))],
        out_specs=pl.BlockSpec((1, tm, tf), lambda b, i, j: (b, i, j)),
        out_shape=jax.ShapeDtypeStruct((g, t, f), BF16),
        scratch_shapes=[pltpu.VMEM((tm + 2 * FFN_HALO, d), BF16)],
        compiler_params=_params("parallel", "parallel", "arbitrary"),
        name="ffn_up_conv_glu",
    )(h, h, h, up, up, w, w)


def _mods(mods_l, rows, d):
    m = mods_l[rows[0]:rows[1]]
    return [m[:, k * d:(k + 1) * d].reshape(-1, 1, d) for k in range(6)]


def _ffn_block(x, mods, norm_g, up, dw, down, layer, seq_len):
    sh2, sc2, g2 = mods
    h = modulate(x, norm_g, sc2, sh2)
    a = ffn_up_conv_glu(h, up, layer, dw, seq_len)
    return matmul_residual([a], down, layer, x, g2)


def kernel(x_prompt, x_sample, cache_k_a, cache_v_a, state_ret_fwd, state_ret_bwd, c, c_ctx, adaln_w, adaln_b, norm1_g, norm2_g, w_in_ab, w_out_ab, q_norm_g, k_norm_g, lambda_q1, lambda_k1, lambda_q2, lambda_k2, subln_g, ret_decay_exp_fwd, ret_decay_exp_bwd, conv_pw1, conv_dw, conv_ln_g, conv_ln_b, conv_pw2, ffn_up, ffn_dw, ffn_down):
    batch, seq, d = x_prompt.shape
    dec_batch, dec_seq, _ = x_sample.shape
    depth = adaln_w.shape[0]
    _, _, past_len, h_a, dk_a = cache_k_a.shape
    dh_a = dk_a // 2
    dv_a = cache_v_a.shape[-1]
    _, _, h_b, dk_b, dv_b = state_ret_fwd.shape
    assert dk_a == LANES and dv_a == LANES and dk_b == LANES and dv_b == LANES
    assert dec_batch + 1 <= SUBLANES
    w_qa, w_va, w_qb, w_vb = h_a * dk_a, h_a * dv_a, h_b * dk_b, h_b * dv_b
    col = {}
    off = 0
    for name, width in (("qa", w_qa), ("ka", w_qa), ("va", w_va), ("qb", w_qb), ("kb", w_qb),
                        ("vb", w_vb), ("gb", w_vb)):
        col[name] = off
        off += width

    conds = jnp.zeros((SUBLANES, d), F32).at[0].set(c_ctx).at[1:1 + dec_batch].set(c)
    mods_all = adaln(conds, adaln_w, adaln_b)

    w_in_bf, w_out_bf = w_in_ab.astype(BF16), w_out_ab.astype(BF16)
    pw1_bf, pw2_bf = conv_pw1.astype(BF16), conv_pw2.astype(BF16)
    up_bf, down_bf = ffn_up.astype(BF16), ffn_down.astype(BF16)

    rope_a = rope_tables(dec_seq, dh_a)
    rope_b = rope_tables(dec_seq, dk_b)

    def run(x, rows, seq_len, latent):
        g, t, _ = x.shape
        n_seq = g * t // seq_len
        new_ctx = []
        for l in range(depth):
            sh1, sc1, g1, sh2, sc2, g2 = _mods(mods_all[l], rows, d)
            h = modulate(x, norm1_g[l], sc1, sh1)
            i = l // 2
            if l % 2 == 0:
                proj = matmul(h, w_in_bf, i, F32)
                ra = rope_a if latent else None
                qa = head_prep(proj, col["qa"], w_qa, gain=q_norm_g[i], group=dh_a, rope=ra,
                               postscale=dh_a ** -0.5 * math.log2(math.e))
                ka = head_prep(proj, col["ka"], w_qa, gain=k_norm_g[i], group=dh_a, rope=ra,
                               want_f32=not latent)
                va = head_prep(proj, col["va"], w_va, want_f32=not latent)
                if latent:
                    qb = head_prep(proj, col["qb"], w_qb, rope=rope_b)
                    kb = head_prep(proj, col["kb"], w_qb, rope=rope_b, prescale=dk_b ** -0.5)
                seqs = lambda z: z.reshape(n_seq, seq_len, z.shape[-1])
                if latent:
                    k_all = jnp.concatenate(
                        [cache_k_a[:, i].reshape(dec_batch, past_len, w_qa).astype(BF16), ka], axis=1)
                    v_all = jnp.concatenate(
                        [cache_v_a[:, i].reshape(dec_batch, past_len, w_va).astype(BF16), va], axis=1)
                    s0f, s0b = state_ret_fwd[:, i], state_ret_bwd[:, i]
                else:
                    ka, ka_f32 = ka
                    va, va_f32 = va
                    k_all, v_all = seqs(ka), seqs(va)
                    s0f = jnp.zeros((n_seq, h_b, dk_b, dv_b), F32)
                    s0b = s0f
                lam_init = 0.8 - 0.6 * math.exp(-0.3 * l)
                lam_params = jnp.stack([lambda_q1[i], lambda_k1[i], lambda_q2[i], lambda_k2[i]]).astype(F32)
                oa = diff_attention(seqs(qa), k_all, v_all, lam_params, subln_g[i], lam_init, dh_a)
                dexp = jnp.stack([ret_decay_exp_fwd[i], ret_decay_exp_bwd[i]], axis=1).astype(F32)
                dexp = jnp.broadcast_to(dexp[:, :, None], (h_b, 2, LANES))
                projs = seqs(proj)
                if latent:
                    qkv = ((seqs(qb), 0), (seqs(kb), 0), (projs, col["vb"]))
                    k_scale = 1.0
                else:
                    qkv = ((projs, col["qb"]), (projs, col["kb"]), (projs, col["vb"]))
                    k_scale = dk_b ** -0.5
                ob, s_f, s_b = bi_retention(*qkv, (projs, col["gb"]), k_scale, h_b, dexp, s0f, s0b)
                x = matmul_residual([oa.reshape(g, t, w_va), ob.reshape(g, t, w_vb)], w_out_bf, i, x, g1)
                if not latent:
                    new_ctx.append((ka_f32.reshape(n_seq, seq_len, h_a, dk_a),
                                    va_f32.reshape(n_seq, seq_len, h_a, dv_a), s_f, s_b))
            else:
                u = matmul_glu(h, pw1_bf, i, F32)
                z = conv_ln_silu(u, conv_dw[i], conv_ln_g[i], conv_ln_b[i], seq_len)
                x = matmul_residual([z], pw2_bf, i, x, g1)
            x = _ffn_block(x, (sh2, sc2, g2), norm2_g[l], up_bf, ffn_dw[l], down_bf, l, seq_len)
        return x, new_ctx

    y_ctx, new_ctx = run(x_prompt.reshape(1, batch * seq, d), (0, 1), seq, False)
    y_lat, _ = run(x_sample, (1, 1 + dec_batch), dec_seq, True)
    stack = lambda k: jnp.stack([n[k] for n in new_ctx], axis=1)
    return (y_ctx.reshape(batch, seq, d), y_lat, stack(0), stack(1), stack(2), stack(3))
```

```python
import functools
import math

import jax
import jax.numpy as jnp
from jax import lax
from jax.experimental import pallas as pl
from jax.experimental.pallas import tpu as pltpu

F32 = jnp.float32
BF16 = jnp.bfloat16
EPS = 1e-6
GRID_W = 64
ROPE_BASE = 10000.0
LANES = 128
SUBLANES = 8
MXU_COLS = 256
CONV_HALO = 16
CONV_ROWS = 128
CONV_COLS = 128
VMEM_LIMIT_V7X = 56 * 1024 * 1024
RET_CHUNK = 256


def _tile(dim, pref, align):
    if dim <= pref:
        return dim
    t = (pref // align) * align
    while t >= align:
        if dim % t == 0:
            return t
        t -= align
    return dim


def _params(*sem):
    return pltpu.CompilerParams(dimension_semantics=sem, vmem_limit_bytes=VMEM_LIMIT_V7X)


def _adaln_kernel(c_ref, w_ref, b_ref, o_ref):
    c = c_ref[...]
    s = (c * jax.nn.sigmoid(c)).astype(BF16)
    w = w_ref[0].astype(BF16)
    o_ref[0] = jnp.dot(s, w, preferred_element_type=F32) + b_ref[0]


def adaln(conds, w, b):
    depth, d, n = w.shape
    tn = _tile(n, 512, LANES)
    return pl.pallas_call(
        _adaln_kernel,
        grid=(depth, n // tn),
        in_specs=[pl.BlockSpec((SUBLANES, d), lambda l, j: (0, 0)),
                  pl.BlockSpec((1, d, tn), lambda l, j: (l, 0, j)),
                  pl.BlockSpec((1, 1, tn), lambda l, j: (l, 0, j))],
        out_specs=pl.BlockSpec((1, SUBLANES, tn), lambda l, j: (l, 0, j)),
        out_shape=jax.ShapeDtypeStruct((depth, SUBLANES, n), F32),
        compiler_params=_params("parallel", "parallel"),
        name="adaln",
    )(conds, w, b.reshape(depth, 1, n))


def _modulate_kernel(x_ref, g_ref, sc_ref, sh_ref, o_ref):
    d = x_ref.shape[2]
    cw = _tile(d, 512, LANES)
    chunks = [slice(c0, c0 + cw) for c0 in range(0, d, cw)]
    ss = sum(jnp.sum(jnp.square(x_ref[0, :, cs]), axis=-1, keepdims=True) for cs in chunks)
    inv = lax.rsqrt(ss * (1.0 / d) + EPS)
    for cs in chunks:
        y = x_ref[0, :, cs] * inv * g_ref[:, cs]
        o_ref[0, :, cs] = (y * (1.0 + sc_ref[0, :, cs]) + sh_ref[0, :, cs]).astype(o_ref.dtype)


def modulate(x, gain, scale, shift):
    g, t, d = x.shape
    tm = _tile(t, 256, SUBLANES)
    return pl.pallas_call(
        _modulate_kernel,
        grid=(g, t // tm),
        in_specs=[pl.BlockSpec((1, tm, d), lambda b, i: (b, i, 0)),
                  pl.BlockSpec((1, d), lambda b, i: (0, 0)),
                  pl.BlockSpec((1, 1, d), lambda b, i: (b, 0, 0)),
                  pl.BlockSpec((1, 1, d), lambda b, i: (b, 0, 0))],
        out_specs=pl.BlockSpec((1, tm, d), lambda b, i: (b, i, 0)),
        out_shape=jax.ShapeDtypeStruct((g, t, d), BF16),
        compiler_params=_params("parallel", "parallel"),
        name="modulate",
    )(x, gain.reshape(1, d), scale, shift)


def _mm_kernel(a_ref, w_ref, o_ref):
    o_ref[0] = jnp.dot(a_ref[0], w_ref[...], preferred_element_type=F32).astype(o_ref.dtype)


def _mm_res_kernel(*refs):
    *a_refs, w_ref, r_ref, g_ref, o_ref = refs
    acc, k0 = None, 0
    for a_ref in a_refs:
        k1 = k0 + a_ref.shape[2]
        part = jnp.dot(a_ref[0], w_ref[k0:k1, :], preferred_element_type=F32)
        acc = part if acc is None else acc + part
        k0 = k1
    o_ref[0] = r_ref[0] + g_ref[0] * acc


def _mm_glu_kernel(a_ref, wa_ref, wg_ref, o_ref):
    a = jnp.dot(a_ref[0], wa_ref[...], preferred_element_type=F32)
    g = jnp.dot(a_ref[0], wg_ref[...], preferred_element_type=F32)
    o_ref[0] = (a * jax.nn.sigmoid(g)).astype(o_ref.dtype)


def _mm_tiles(t, k, n):
    tm = _tile(t, 1024, SUBLANES)
    tn = _tile(n, 512 if k <= 8192 else 256, LANES)
    return tm, tn


def _lhs_spec(tm, k_piece, k_total):
    mode = pl.Buffered(1) if k_total > 8192 else None
    return pl.BlockSpec((1, tm, k_piece), lambda b, i, j: (b, i, 0), pipeline_mode=mode)


def _w_spec(k, tn, layer, col_block_offset=0):
    return pl.BlockSpec((None, k, tn), lambda b, i, j: (layer, 0, j + col_block_offset))


def matmul(a, w, layer, out_dtype):
    g, t, k = a.shape
    n = w.shape[2]
    tm, tn = _mm_tiles(t, k, n)
    return pl.pallas_call(
        _mm_kernel,
        grid=(g, t // tm, n // tn),
        in_specs=[pl.BlockSpec((1, tm, k), lambda b, i, j: (b, i, 0)),
                  _w_spec(k, tn, layer)],
        out_specs=pl.BlockSpec((1, tm, tn), lambda b, i, j: (b, i, j)),
        out_shape=jax.ShapeDtypeStruct((g, t, n), out_dtype),
        compiler_params=_params("parallel", "parallel", "parallel"),
        name="matmul",
    )(a, w)


def matmul_residual(a_pieces, w, layer, res, gate):
    g, t, _ = a_pieces[0].shape
    k = sum(a.shape[2] for a in a_pieces)
    n = w.shape[2]
    assert k == w.shape[1]
    tm, tn = _mm_tiles(t, k, n)
    return pl.pallas_call(
        _mm_res_kernel,
        grid=(g, t // tm, n // tn),
        in_specs=[_lhs_spec(tm, a.shape[2], k) for a in a_pieces]
                 + [_w_spec(k, tn, layer),
                    pl.BlockSpec((1, tm, tn), lambda b, i, j: (b, i, j)),
                    pl.BlockSpec((1, 1, tn), lambda b, i, j: (b, 0, j))],
        out_specs=pl.BlockSpec((1, tm, tn), lambda b, i, j: (b, i, j)),
        out_shape=jax.ShapeDtypeStruct((g, t, n), F32),
        compiler_params=_params("parallel", "parallel", "parallel"),
        name="matmul_residual",
    )(*a_pieces, w, res, gate)


def matmul_glu(a, w, layer, out_dtype):
    g, t, k = a.shape
    n = w.shape[2] // 2
    tm, tn = _mm_tiles(t, k, n)
    nj = n // tn
    return pl.pallas_call(
        _mm_glu_kernel,
        grid=(g, t // tm, nj),
        in_specs=[pl.BlockSpec((1, tm, k), lambda b, i, j: (b, i, 0)),
                  _w_spec(k, tn, layer),
                  _w_spec(k, tn, layer, nj)],
        out_specs=pl.BlockSpec((1, tm, tn), lambda b, i, j: (b, i, j)),
        out_shape=jax.ShapeDtypeStruct((g, t, n), out_dtype),
        compiler_params=_params("parallel", "parallel", "parallel"),
        name="matmul_glu",
    )(a, w, w)


def _prep_kernel(*refs, norm, rope_q, prescale, postscale, want_f32, group):
    refs = list(refs)
    x_ref = refs.pop(0)
    if norm:
        g_ref = refs.pop(0)
        bd_ref = refs.pop(0)
    if rope_q:
        cos_ref = refs.pop(0)
        sa_ref = refs.pop(0)
        sb_ref = refs.pop(0)
    o_ref = refs.pop(0)
    f_ref = refs.pop(0) if want_f32 else None
    for hh in range(x_ref.shape[2] // LANES):
        cols = slice(hh * LANES, (hh + 1) * LANES)
        y = x_ref[0, :, cols].astype(F32)
        if norm:
            sq = y * y
            hi = sq.astype(BF16)
            lo = (sq - hi.astype(F32)).astype(BF16)
            ss = (jnp.dot(hi, bd_ref[...], preferred_element_type=F32)
                  + jnp.dot(lo, bd_ref[...], preferred_element_type=F32))
            y = y * lax.rsqrt(ss * (1.0 / group) + EPS) * g_ref[...]
        if prescale != 1.0:
            y = y * prescale
        if rope_q:
            y = (y * cos_ref[...]
                 + pltpu.roll(y, LANES - rope_q, axis=1) * sa_ref[...]
                 + pltpu.roll(y, rope_q, axis=1) * sb_ref[...])
        if want_f32:
            f_ref[0, :, cols] = y
        if postscale != 1.0:
            y = y * postscale
        o_ref[0, :, cols] = y.astype(o_ref.dtype)


def head_prep(proj, col_start, width, *, gain=None, group=None, rope=None, prescale=1.0,
              postscale=1.0, want_f32=False):
    g, t, _ = proj.shape
    cw = _tile(width, 1024, LANES)
    tm = _tile(t, 512, SUBLANES)
    c0 = col_start // cw
    assert col_start % cw == 0
    in_specs = [pl.BlockSpec((1, tm, cw), lambda b, i, j: (b, i, c0 + j))]
    args = [proj]
    if gain is not None:
        lane = jnp.arange(LANES)
        bd = (lane[:, None] // group == lane[None, :] // group).astype(BF16)
        in_specs += [pl.BlockSpec((1, LANES), lambda b, i, j: (0, 0)),
                     pl.BlockSpec((LANES, LANES), lambda b, i, j: (0, 0))]
        args += [jnp.tile(gain.astype(F32), LANES // group).reshape(1, LANES), bd]
    rope_q = 0
    if rope is not None:
        rope_q, cos, sin_a, sin_b = rope
        in_specs += [pl.BlockSpec((tm, LANES), lambda b, i, j: (i, 0))] * 3
        args += [cos, sin_a, sin_b]
    out_specs = [pl.BlockSpec((1, tm, cw), lambda b, i, j: (b, i, j))]
    out_shape = [jax.ShapeDtypeStruct((g, t, width), BF16)]
    if want_f32:
        out_specs.append(pl.BlockSpec((1, tm, cw), lambda b, i, j: (b, i, j)))
        out_shape.append(jax.ShapeDtypeStruct((g, t, width), F32))
    outs = pl.pallas_call(
        functools.partial(_prep_kernel, norm=gain is not None, rope_q=rope_q, prescale=prescale,
                          postscale=postscale, want_f32=want_f32, group=group),
        grid=(g, t // tm, width // cw),
        in_specs=in_specs,
        out_specs=out_specs,
        out_shape=out_shape,
        compiler_params=_params("parallel", "parallel", "parallel"),
        name="head_prep",
    )(*args)
    return outs if want_f32 else outs[0]


def rope_tables(n_tokens, dim):
    rows = n_tokens // GRID_W
    row = jnp.repeat(jnp.arange(rows, dtype=F32), GRID_W)
    col = jnp.tile(jnp.arange(GRID_W, dtype=F32), rows)
    n_freq = dim // 4
    inv = ROPE_BASE ** (-jnp.arange(n_freq, dtype=F32) / n_freq)
    ar = row[:, None] * inv
    ac = col[:, None] * inv
    ang = jnp.concatenate([ar, ar, ac, ac], axis=-1)
    cos, sin = jnp.cos(ang), jnp.sin(ang)
    even = ((jnp.arange(dim) // n_freq) % 2 == 0)[None, :]
    sin_a = jnp.where(even, -sin, 0.0)
    sin_b = jnp.where(even, 0.0, sin)
    rep = LANES // dim
    return n_freq, jnp.tile(cos, (1, rep)), jnp.tile(sin_a, (1, rep)), jnp.tile(sin_b, (1, rep))


ATTN_ROWS = 128
ATTN_KEYS = 256


def _attn_kernel(q_ref, k_ref, v_ref, lp_ref, g_ref, o_ref, p_ref, s_ref, *, lam_init, dh):
    lk = k_ref.shape[1]
    tq = min(q_ref.shape[1], ATTN_ROWS)
    tk = _tile(lk, ATTN_KEYS, LANES)
    lp = lp_ref[...]
    lam = (jnp.exp(jnp.sum(lp[0:1] * lp[1:2], axis=1, keepdims=True))
           - jnp.exp(jnp.sum(lp[2:3] * lp[3:4], axis=1, keepdims=True)) + lam_init)
    dim1 = (((1,), (1,)), ((), ()))
    lane_tiles = [slice(j, j + LANES) for j in range(0, tk, LANES)]
    tile = 0
    for hh in range(q_ref.shape[2] // LANES):
        cols = slice(hh * LANES, (hh + 1) * LANES)
        for q0 in range(0, q_ref.shape[1], tq):
            pb_ref = p_ref.at[tile % 2]
            sb_ref = s_ref.at[tile % 2]
            tile += 1
            q = q_ref[0, q0:q0 + tq, cols]
            lane = lax.broadcasted_iota(jnp.int32, q.shape, 1)
            zero = jnp.zeros_like(q)
            qs = jnp.concatenate([jnp.where(lane < dh, q, zero), jnp.where(lane >= dh, q, zero)], axis=0)
            mx = jnp.full((2 * tq, LANES), -jnp.inf, F32)
            for k0 in range(0, lk, tk):
                s = lax.dot_general(qs, k_ref[0, k0:k0 + tk, cols], dim1, preferred_element_type=F32)
                sb_ref[:, k0:k0 + tk] = s
                for lt in lane_tiles:
                    mx = jnp.maximum(mx, s[:, lt])
            m = jnp.broadcast_to(jnp.max(mx, axis=1, keepdims=True), (2 * tq, LANES))
            ls = jnp.zeros((2 * tq, LANES), F32)
            for k0 in range(0, lk, LANES):
                p = jnp.exp2(sb_ref[:, k0:k0 + LANES] - m)
                ls = ls + p
                pb_ref[:, k0:k0 + LANES] = p.astype(BF16)
            l = jnp.sum(ls, axis=1, keepdims=True)
            on = jnp.dot(pb_ref[...], v_ref[0, :, cols], preferred_element_type=F32) / l
            o = on[0:tq] - lam * on[tq:2 * tq]
            ms = jnp.mean(o * o, axis=-1, keepdims=True)
            o = o * lax.rsqrt(ms + EPS) * g_ref[...] * (1.0 - lam_init)
            o_ref[0, q0:q0 + tq, cols] = o.astype(o_ref.dtype)


def diff_attention(q, k, v, lam_params, subln_g, lam_init, dh):
    b, lq, w = q.shape
    lk = k.shape[1]
    h = w // LANES
    tq = _tile(lq, 4 * ATTN_ROWS, SUBLANES)
    hb = _tile(h, 4, 1) if lk <= 2 * ATTN_KEYS else 1
    wb = hb * LANES
    return pl.pallas_call(
        functools.partial(_attn_kernel, lam_init=lam_init, dh=dh),
        grid=(b, h // hb, lq // tq),
        in_specs=[pl.BlockSpec((1, tq, wb), lambda bi, hi, qi: (bi, qi, hi)),
                  pl.BlockSpec((1, lk, wb), lambda bi, hi, qi: (bi, 0, hi)),
                  pl.BlockSpec((1, lk, wb), lambda bi, hi, qi: (bi, 0, hi)),
                  pl.BlockSpec((4, dh), lambda bi, hi, qi: (0, 0)),
                  pl.BlockSpec((1, LANES), lambda bi, hi, qi: (0, 0))],
        out_specs=pl.BlockSpec((1, tq, wb), lambda bi, hi, qi: (bi, qi, hi)),
        out_shape=jax.ShapeDtypeStruct((b, lq, w), BF16),
        scratch_shapes=[pltpu.VMEM((2, 2 * min(tq, ATTN_ROWS), lk), BF16),
                        pltpu.VMEM((2, 2 * min(tq, ATTN_ROWS), lk), F32)],
        compiler_params=_params("parallel", "parallel", "parallel"),
        name="diff_attention",
    )(q, k, v, lam_params, subln_g.astype(F32).reshape(1, LANES))


def _retention_kernel(q_ref, k_ref, v_ref, gate_ref, dx_ref, s0f_ref, s0b_ref,
                      o_ref, sf_ref, sb_ref, acc_ref, accb_ref, *, chunk, k_scale):
    seq = q_ref.shape[1]
    n_chunks = seq // chunk
    dx = dx_ref[0]
    lg = jnp.log1p(-jnp.exp2(-dx))
    lgf, lgb = lg[0:1], lg[1:2]
    ii = lax.broadcasted_iota(jnp.int32, (chunk, chunk), 0)
    jj = lax.broadcasted_iota(jnp.int32, (chunk, chunk), 1)
    d = (ii - jj).astype(F32)
    dmat = (jnp.where(d >= 0, jnp.exp(jnp.maximum(d, 0.0) * lgf[:, 0:1]), 0.0)
            + jnp.where(d <= 0, jnp.exp(jnp.maximum(-d, 0.0) * lgb[:, 0:1]), 0.0))
    pos = lax.broadcasted_iota(jnp.int32, (chunk, LANES), 0).astype(F32)
    qdec_f = jnp.exp((pos + 1.0) * lgf)
    kdec_f = jnp.exp((chunk - 1.0 - pos) * lgf)
    cdec_f = jnp.exp(chunk * lgf)
    qdec_b = jnp.exp((chunk - pos) * lgb)
    kdec_b = jnp.exp(pos * lgb)
    cdec_b = jnp.exp(chunk * lgb)
    dim0 = (((0,), (0,)), ((), ()))
    dim1 = (((1,), (1,)), ((), ()))

    def load(n):
        rows = pl.ds(pl.multiple_of(n * chunk, chunk), chunk)
        k = k_ref[0, rows, :].astype(F32)
        if k_scale != 1.0:
            k = k * k_scale
        return rows, q_ref[0, rows, :].astype(F32), k, v_ref[0, rows, :].astype(BF16)

    def step(n, carry):
        s_f, s_b = carry
        rows, q, k, v = load(n)
        att = lax.dot_general(q.astype(BF16), k.astype(BF16), dim1, preferred_element_type=F32) * dmat
        acc_ref[rows, :] = (jnp.dot(att.astype(BF16), v, preferred_element_type=F32)
                            + jnp.dot((q * qdec_f).astype(BF16), s_f.astype(BF16),
                                      preferred_element_type=F32))
        s_f = s_f * cdec_f + lax.dot_general((k * kdec_f).astype(BF16), v, dim0,
                                             preferred_element_type=F32)
        rows, q, k, v = load(n_chunks - 1 - n)
        accb_ref[rows, :] = jnp.dot((q * qdec_b).astype(BF16), s_b.astype(BF16),
                                    preferred_element_type=F32)
        s_b = s_b * cdec_b + lax.dot_general((k * kdec_b).astype(BF16), v, dim0,
                                             preferred_element_type=F32)
        return s_f, s_b

    sf_ref[0, 0], sb_ref[0, 0] = lax.fori_loop(0, n_chunks, step, (s0f_ref[0, 0], s0b_ref[0, 0]))

    o = acc_ref[...] + accb_ref[...]
    ms = jnp.mean(o * o, axis=-1, keepdims=True)
    gate = gate_ref[0].astype(F32)
    o_ref[0] = (o * lax.rsqrt(ms + EPS) * (gate * jax.nn.sigmoid(gate))).astype(o_ref.dtype)


def bi_retention(q, k, v, gate, k_scale, h, dexp, s0f, s0b):
    b, seq, _ = q[0].shape
    w = h * LANES
    chunk = _tile(seq, RET_CHUNK, LANES)
    head = lambda bi, hi: (bi, 0, hi)
    state = lambda bi, hi: (bi, hi, 0, 0)
    cols = lambda src: pl.BlockSpec((1, seq, LANES), lambda bi, hi, c0=src[1] // LANES: (bi, 0, c0 + hi))
    return pl.pallas_call(
        functools.partial(_retention_kernel, chunk=chunk, k_scale=k_scale),
        grid=(b, h),
        in_specs=[cols(q), cols(k), cols(v), cols(gate),
                  pl.BlockSpec((1, 2, LANES), lambda bi, hi: (hi, 0, 0)),
                  pl.BlockSpec((1, 1, LANES, LANES), state),
                  pl.BlockSpec((1, 1, LANES, LANES), state)],
        out_specs=[pl.BlockSpec((1, seq, LANES), head),
                   pl.BlockSpec((1, 1, LANES, LANES), state),
                   pl.BlockSpec((1, 1, LANES, LANES), state)],
        out_shape=[jax.ShapeDtypeStruct((b, seq, w), BF16),
                   jax.ShapeDtypeStruct((b, h, LANES, LANES), F32),
                   jax.ShapeDtypeStruct((b, h, LANES, LANES), F32)],
        scratch_shapes=[pltpu.VMEM((seq, LANES), F32), pltpu.VMEM((seq, LANES), F32)],
        compiler_params=_params("parallel", "parallel"),
        name="bi_retention",
    )(q[0], k[0], v[0], gate[0], dexp, s0f, s0b)


def _conv_ln_kernel(u_ref, prev_ref, next_ref, w_ref, g_ref, b_ref, o_ref, ext_ref, y_ref, shift_ref, *,
                    seq_len, taps, row_block):
    i = pl.program_id(1)
    c = pl.program_id(2)
    tm, cw = u_ref.shape[1], u_ref.shape[2]
    first = (i * tm) % seq_len == 0
    last = ((i + 1) * tm) % seq_len == 0
    zero = jnp.zeros((CONV_HALO, cw), F32)
    ext_ref[0:CONV_HALO, :] = jnp.where(first, zero, prev_ref[0])
    ext_ref[CONV_HALO:CONV_HALO + tm, :] = u_ref[0]
    ext_ref[CONV_HALO + tm:, :] = jnp.where(last, zero, next_ref[0])
    half = (taps - 1) // 2
    sw = shift_ref.shape[2]
    for c0 in range(0, cw, sw):
        cols = slice(c0, c0 + sw)
        for r0 in range(0, tm, row_block):
            acc = None
            for b in range(SUBLANES):
                part = None
                for a in range(-((half + b) // SUBLANES), (half - b) // SUBLANES + 1):
                    o = b + SUBLANES * a
                    term = (ext_ref[pl.ds(CONV_HALO + r0 + SUBLANES * a, row_block + SUBLANES), cols]
                            * w_ref[half + o:half + o + 1, cols])
                    part = term if part is None else part + term
                if b == 0:
                    shifted = part[0:row_block]
                else:
                    shift_ref[b % 2] = part
                    shifted = shift_ref[b % 2, pl.ds(b, row_block), :]
                acc = shifted if acc is None else acc + shifted
            y_ref[c, pl.ds(r0, row_block), cols] = acc

    @pl.when(c == pl.num_programs(2) - 1)
    def _():
        nc = y_ref.shape[0]
        d = nc * cw
        mu = sum(jnp.sum(y_ref[cc], axis=-1, keepdims=True) for cc in range(nc)) * (1.0 / d)
        var = sum(jnp.sum(jnp.square(y_ref[cc] - mu), axis=-1, keepdims=True)
                  for cc in range(nc)) * (1.0 / d)
        inv = lax.rsqrt(var + EPS)
        for cc in range(nc):
            cols = slice(cc * cw, (cc + 1) * cw)
            z = (y_ref[cc] - mu) * inv * g_ref[:, cols] + b_ref[:, cols]
            o_ref[0, :, cols] = (z * jax.nn.sigmoid(z)).astype(o_ref.dtype)


def conv_ln_silu(u, dw, ln_g, ln_b, seq_len):
    g, t, d = u.shape
    taps = dw.shape[0]
    half = (taps - 1) // 2
    assert taps % 2 == 1 and SUBLANES * (-(-half // SUBLANES)) <= CONV_HALO
    assert SUBLANES * (half // SUBLANES + 1) <= CONV_HALO
    tm = _tile(min(t, seq_len), 256, CONV_HALO)
    cw = _tile(d, 512, LANES)
    row_block = min(tm, CONV_ROWS)
    sw = _tile(cw, CONV_COLS, LANES)
    nh = tm // CONV_HALO
    n_halo_blocks = t // CONV_HALO
    w = jnp.zeros((2 * CONV_HALO, d), F32).at[:taps].set(dw.astype(F32))
    return pl.pallas_call(
        functools.partial(_conv_ln_kernel, seq_len=seq_len, taps=taps, row_block=row_block),
        grid=(g, t // tm, d // cw),
        in_specs=[pl.BlockSpec((1, tm, cw), lambda b, i, c: (b, i, c)),
                  pl.BlockSpec((1, CONV_HALO, cw), lambda b, i, c: (b, jnp.maximum(i * nh - 1, 0), c)),
                  pl.BlockSpec((1, CONV_HALO, cw),
                               lambda b, i, c: (b, jnp.minimum((i + 1) * nh, n_halo_blocks - 1), c)),
                  pl.BlockSpec((2 * CONV_HALO, cw), lambda b, i, c: (0, c)),
                  pl.BlockSpec((1, d), lambda b, i, c: (0, 0)),
                  pl.BlockSpec((1, d), lambda b, i, c: (0, 0))],
        out_specs=pl.BlockSpec((1, tm, d), lambda b, i, c: (b, i, 0)),
        out_shape=jax.ShapeDtypeStruct((g, t, d), BF16),
        scratch_shapes=[pltpu.VMEM((tm + 2 * CONV_HALO, cw), F32),
                        pltpu.VMEM((d // cw, tm, cw), F32),
                        pltpu.VMEM((2, row_block + SUBLANES, sw), F32)],
        compiler_params=_params("parallel", "parallel", "arbitrary"),
        name="conv_ln_silu",
    )(u, u, u, w, ln_g.astype(F32).reshape(1, d), ln_b.astype(F32).reshape(1, d))


FFN_HALO = 16


def _patch_rows(arr, patches):
    sub = lax.broadcasted_iota(jnp.int32, (SUBLANES, arr.shape[1]), 0)
    pieces, done = [], 0
    for r, val in patches:
        g0 = r // SUBLANES * SUBLANES
        assert g0 >= done
        if g0 > done:
            pieces.append(arr[done:g0])
        pieces.append(jnp.where(sub == r % SUBLANES, val, arr[g0:g0 + SUBLANES]))
        done = g0 + SUBLANES
    if done < arr.shape[0]:
        pieces.append(arr[done:])
    return jnp.concatenate(pieces, axis=0)


def _up_glu_kernel(h_ref, hp_ref, hn_ref, wg_ref, wv_ref, dg_ref, dv_ref, o_ref, hx_ref, *, seq_len):
    i = pl.program_id(1)
    tm = h_ref.shape[1]
    halo = FFN_HALO

    @pl.when(pl.program_id(2) == 0)
    def _():
        hx_ref[0:halo, :] = hp_ref[0]
        hx_ref[halo:halo + tm, :] = h_ref[0]
        hx_ref[halo + tm:, :] = hn_ref[0]

    cw = MXU_COLS if o_ref.shape[2] % MXU_COLS == 0 else LANES
    tile_in_one_seq = seq_len % tm == 0
    if tile_in_one_seq:
        first = lax.rem(i * tm, seq_len) == 0
        last = lax.rem((i + 1) * tm, seq_len) == 0

    def conv(w_ref, d_ref, cols):
        p = jnp.dot(hx_ref[...], w_ref[:, cols], preferred_element_type=F32)
        x = p[halo:halo + tm]
        if tile_in_one_seq:
            starts = [(0, jnp.where(first, 0.0, p[halo - 1:halo]))]
            ends = [(tm - 1, jnp.where(last, 0.0, p[halo + tm:halo + tm + 1]))]
        else:
            starts = [(r, 0.0) for r in range(0, tm, seq_len)]
            ends = [(r + seq_len - 1, 0.0) for r in range(0, tm, seq_len)]
        xm = _patch_rows(pltpu.roll(x, 1, axis=0), starts)
        xp = _patch_rows(pltpu.roll(x, tm - 1, axis=0), ends)
        d = d_ref[:, cols]
        return xm * d[0:1] + x * d[1:2] + xp * d[2:3]

    for c0 in range(0, o_ref.shape[2], cw):
        cols = slice(c0, c0 + cw)
        gate = conv(wg_ref, dg_ref, cols)
        val = conv(wv_ref, dv_ref, cols)
        o_ref[0, :, cols] = (gate * jax.nn.sigmoid(gate) * val).astype(o_ref.dtype)


def ffn_up_conv_glu(h, up, layer, dw, seq_len):
    g, t, d = h.shape
    f = up.shape[2] // 2
    assert dw.shape[0] == 3
    tm = _tile(t, 1024, FFN_HALO)
    assert seq_len % SUBLANES == 0 and (seq_len % tm == 0 or tm % seq_len == 0)
    tf = _tile(f, 512, LANES)
    nj = f // tf
    nh = tm // FFN_HALO
    n_halo_blocks = t // FFN_HALO
    w = jnp.zeros((SUBLANES, 2 * f), F32).at[:3].set(dw.astype(F32))
    return pl.pallas_call(
        functools.partial(_up_glu_kernel, seq_len=seq_len),
        grid=(g, t // tm, nj),
        in_specs=[pl.BlockSpec((1, tm, d), lambda b, i, j: (b, i, 0), pipeline_mode=pl.Buffered(1)),
                  pl.BlockSpec((1, FFN_HALO, d), lambda b, i, j: (b, jnp.maximum(i * nh - 1, 0), 0)),
                  pl.BlockSpec((1, FFN_HALO, d),
                               lambda b, i, j: (b, jnp.minimum((i + 1) * nh, n_halo_blocks - 1), 0)),
                  _w_spec(d, tf, layer),
                  _w_spec(d, tf, layer, nj),
                  pl.BlockSpec((SUBLANES, tf), lambda b, i, j: (0, j)),
                  pl.BlockSpec((SUBLANES, tf), lambda b, i, j: (0, j + nj))],
        out_specs=pl.BlockSpec((1, tm, tf), lambda b, i, j: (b, i, j)),
        out_shape=jax.ShapeDtypeStruct((g, t, f), BF16),
        scratch_shapes=[pltpu.VMEM((tm + 2 * FFN_HALO, d), BF16)],
        compiler_params=_params("parallel", "parallel", "arbitrary"),
        name="ffn_up_conv_glu",
    )(h, h, h, up, up, w, w)


def _mods(mods_l, rows, d):
    m = mods_l[rows[0]:rows[1]]
    return [m[:, k * d:(k + 1) * d].reshape(-1, 1, d) for k in range(6)]


def _ffn_block(x, mods, norm_g, up, dw, down, layer, seq_len):
    sh2, sc2, g2 = mods
    h = modulate(x, norm_g, sc2, sh2)
    a = ffn_up_conv_glu(h, up, layer, dw, seq_len)
    return matmul_residual([a], down, layer, x, g2)


def kernel(x_prompt, x_sample, cache_k_a, cache_v_a, state_ret_fwd, state_ret_bwd, c, c_ctx, adaln_w, adaln_b, norm1_g, norm2_g, w_in_ab, w_out_ab, q_norm_g, k_norm_g, lambda_q1, lambda_k1, lambda_q2, lambda_k2, subln_g, ret_decay_exp_fwd, ret_decay_exp_bwd, conv_pw1, conv_dw, conv_ln_g, conv_ln_b, conv_pw2, ffn_up, ffn_dw, ffn_down):
    batch, seq, d = x_prompt.shape
    dec_batch, dec_seq, _ = x_sample.shape
    depth = adaln_w.shape[0]
    _, _, past_len, h_a, dk_a = cache_k_a.shape
    dh_a = dk_a // 2
    dv_a = cache_v_a.shape[-1]
    _, _, h_b, dk_b, dv_b = state_ret_fwd.shape
    assert dk_a == LANES and dv_a == LANES and dk_b == LANES and dv_b == LANES
    assert dec_batch + 1 <= SUBLANES
    w_qa, w_va, w_qb, w_vb = h_a * dk_a, h_a * dv_a, h_b * dk_b, h_b * dv_b
    col = {}
    off = 0
    for name, width in (("qa", w_qa), ("ka", w_qa), ("va", w_va), ("qb", w_qb), ("kb", w_qb),
                        ("vb", w_vb), ("gb", w_vb)):
        col[name] = off
        off += width

    conds = jnp.zeros((SUBLANES, d), F32).at[0].set(c_ctx).at[1:1 + dec_batch].set(c)
    mods_all = adaln(conds, adaln_w, adaln_b)

    w_in_bf, w_out_bf = w_in_ab.astype(BF16), w_out_ab.astype(BF16)
    pw1_bf, pw2_bf = conv_pw1.astype(BF16), conv_pw2.astype(BF16)
    up_bf, down_bf = ffn_up.astype(BF16), ffn_down.astype(BF16)

    rope_a = rope_tables(dec_seq, dh_a)
    rope_b = rope_tables(dec_seq, dk_b)

    def run(x, rows, seq_len, latent):
        g, t, _ = x.shape
        n_seq = g * t // seq_len
        new_ctx = []
        for l in range(depth):
            sh1, sc1, g1, sh2, sc2, g2 = _mods(mods_all[l], rows, d)
            h = modulate(x, norm1_g[l], sc1, sh1)
            i = l // 2
            if l % 2 == 0:
                proj = matmul(h, w_in_bf, i, F32)
                ra = rope_a if latent else None
                qa = head_prep(proj, col["qa"], w_qa, gain=q_norm_g[i], group=dh_a, rope=ra,
                               postscale=dh_a ** -0.5 * math.log2(math.e))
                ka = head_prep(proj, col["ka"], w_qa, gain=k_norm_g[i], group=dh_a, rope=ra,
                               want_f32=not latent)
                va = head_prep(proj, col["va"], w_va, want_f32=not latent)
                if latent:
                    qb = head_prep(proj, col["qb"], w_qb, rope=rope_b)
                    kb = head_prep(proj, col["kb"], w_qb, rope=rope_b, prescale=dk_b ** -0.5)
                seqs = lambda z: z.reshape(n_seq, seq_len, z.shape[-1])
                if latent:
                    k_all = jnp.concatenate(
                        [cache_k_a[:, i].reshape(dec_batch, past_len, w_qa).astype(BF16), ka], axis=1)
                    v_all = jnp.concatenate(
                        [cache_v_a[:, i].reshape(dec_batch, past_len, w_va).astype(BF16), va], axis=1)
                    s0f, s0b = state_ret_fwd[:, i], state_ret_bwd[:, i]
                else:
                    ka, ka_f32 = ka
                    va, va_f32 = va
                    k_all, v_all = seqs(ka), seqs(va)
                    s0f = jnp.zeros((n_seq, h_b, dk_b, dv_b), F32)
                    s0b = s0f
                lam_init = 0.8 - 0.6 * math.exp(-0.3 * l)
                lam_params = jnp.stack([lambda_q1[i], lambda_k1[i], lambda_q2[i], lambda_k2[i]]).astype(F32)
                oa = diff_attention(seqs(qa), k_all, v_all, lam_params, subln_g[i], lam_init, dh_a)
                dexp = jnp.stack([ret_decay_exp_fwd[i], ret_decay_exp_bwd[i]], axis=1).astype(F32)
                dexp = jnp.broadcast_to(dexp[:, :, None], (h_b, 2, LANES))
                projs = seqs(proj)
                if latent:
                    qkv = ((seqs(qb), 0), (seqs(kb), 0), (projs, col["vb"]))
                    k_scale = 1.0
                else:
                    qkv = ((projs, col["qb"]), (projs, col["kb"]), (projs, col["vb"]))
                    k_scale = dk_b ** -0.5
                ob, s_f, s_b = bi_retention(*qkv, (projs, col["gb"]), k_scale, h_b, dexp, s0f, s0b)
                x = matmul_residual([oa.reshape(g, t, w_va), ob.reshape(g, t, w_vb)], w_out_bf, i, x, g1)
                if not latent:
                    new_ctx.append((ka_f32.reshape(n_seq, seq_len, h_a, dk_a),
                                    va_f32.reshape(n_seq, seq_len, h_a, dv_a), s_f, s_b))
            else:
                u = matmul_glu(h, pw1_bf, i, F32)
                z = conv_ln_silu(u, conv_dw[i], conv_ln_g[i], conv_ln_b[i], seq_len)
                x = matmul_residual([z], pw2_bf, i, x, g1)
            x = _ffn_block(x, (sh2, sc2, g2), norm2_g[l], up_bf, ffn_dw[l], down_bf, l, seq_len)
        return x, new_ctx

    y_ctx, new_ctx = run(x_prompt.reshape(1, batch * seq, d), (0, 1), seq, False)
    y_lat, _ = run(x_sample, (1, 1 + dec_batch), dec_seq, True)
    stack = lambda k: jnp.stack([n[k] for n in new_ctx], axis=1)
    return (y_ctx.reshape(batch, seq, d), y_lat, stack(0), stack(1), stack(2), stack(3))
```
